```python
import math
import jax, jax.numpy as jnp
from jax import lax
import numpy as np


D_MODEL = 1024
BATCH = 16
SEQ = 2048
DEPTH = 1

ATT_HEADS = 8
ATT_HEAD_DIM = 64
IDX_HEADS = 8
IDX_DIM = 64
TOPK_MAX = 256
Q_BLOCK = 128
ML_HEADS = 4
ML_HEAD_DIM = 128
ML_CHUNK = 64
CONV_WIDTH = 4
D_FF = 4 * D_MODEL
PLE_DIM = 256
ROPE_THETA = 10000.0
LN_EPS = 1e-5
DEEPNORM_ALPHA = (2.0 * DEPTH) ** 0.25
DEEPNORM_BETA = (8.0 * DEPTH) ** -0.25
IDX_W_SCALE = (IDX_HEADS ** -0.5) * (IDX_DIM ** -0.5)

ATT_W = ATT_HEADS * ATT_HEAD_DIM
IDX_QW = IDX_HEADS * IDX_DIM
ML_W = ML_HEADS * ML_HEAD_DIM
SPLIT_SPEC = (
    ('att_q', ATT_W), ('att_k', ATT_HEAD_DIM), ('att_v', ATT_HEAD_DIM),
    ('idx_q', IDX_QW), ('idx_k', IDX_DIM), ('idx_w', IDX_HEADS),
    ('ml_q', ML_W), ('ml_k', ML_W), ('ml_v', ML_W),
    ('ml_i', ML_HEADS), ('ml_f', ML_HEADS), ('ml_o', ML_W),
    ('gate_a', D_MODEL), ('gate_b', D_MODEL),
)
SPLIT_NAMES = tuple(n for n, _ in SPLIT_SPEC)
SPLIT_OFFSETS = tuple(int(o) for o in np.cumsum([w for _, w in SPLIT_SPEC])[:-1])
W_IN_COLS = sum(w for _, w in SPLIT_SPEC)

kernel_name = 'hybrid_dsa_mlstm_block'


def layer_norm(x, g, b):
    xf = x.astype(jnp.float32)
    mu = jnp.mean(xf, axis=-1, keepdims=True)
    var = jnp.mean(jnp.square(xf - mu), axis=-1, keepdims=True)
    y = (xf - mu) * lax.rsqrt(var + LN_EPS) * g.astype(jnp.float32) + b.astype(jnp.float32)
    return y.astype(x.dtype)


def rope_tables(positions, dim):
    inv_freq = 1.0 / (ROPE_THETA ** (jnp.arange(0, dim, 2, dtype=jnp.float32) / dim))
    ang = positions.astype(jnp.float32)[..., None] * inv_freq
    return jnp.cos(ang), jnp.sin(ang)


def apply_rope(x, cos, sin):
    xf = x.astype(jnp.float32)
    x1, x2 = jnp.split(xf, 2, axis=-1)
    c = cos[:, :, None, :]
    s = sin[:, :, None, :]
    return jnp.concatenate([x1 * c - x2 * s, x2 * c + x1 * s], axis=-1).astype(x.dtype)


def dsa_attention(q, k, v, qi, ki, wi):
    B, S = q.shape[0], q.shape[1]
    k_sel = min(TOPK_MAX, S // 4)
    nb = S // Q_BLOCK
    kv = jnp.concatenate([k, v], axis=-1)
    ki32 = ki.astype(jnp.float32)
    key_pos = jnp.arange(S)

    def to_blocks(a):
        return jnp.moveaxis(a.reshape((B, nb, Q_BLOCK) + a.shape[2:]), 1, 0)

    def block(args):
        qb, qib, wb, t0 = args
        tq = t0 + jnp.arange(Q_BLOCK)
        causal = key_pos[None, :] <= tq[:, None]
        logits = jnp.einsum('bthd,bsd->bths', qib.astype(jnp.float32), ki32)
        score = jnp.einsum('bths,bth->bts', jax.nn.relu(logits), wb.astype(jnp.float32) * IDX_W_SCALE)
        score = jnp.where(causal[None], score, -jnp.inf)
        _, idx = lax.top_k(score, k_sel)
        kvg = jax.vmap(lambda a, i: a[i])(kv, idx)
        kg, vg = jnp.split(kvg, 2, axis=-1)
        valid = idx <= tq[None, :, None]
        s = jnp.einsum('bthd,btkd->bthk', qb, kg).astype(jnp.float32) * (ATT_HEAD_DIM ** -0.5)
        s = jnp.where(valid[:, :, None, :], s, -jnp.inf)
        pr = jax.nn.softmax(s, axis=-1)
        return jnp.einsum('bthk,btkd->bthd', pr.astype(vg.dtype), vg)

    out = lax.map(block, (to_blocks(q), to_blocks(qi), to_blocks(wi), jnp.arange(nb) * Q_BLOCK))
    return jnp.moveaxis(out, 0, 1).reshape(B, S, ATT_W)


def causal_dwconv(x, w, b):
    C = x.shape[-1]
    y = lax.conv_general_dilated(x, w[:, None, :], window_strides=(1,), padding=[(CONV_WIDTH - 1, 0)],
                                 dimension_numbers=('NWC', 'WIO', 'NWC'), feature_group_count=C)
    return y + b


def mlstm(q, k, v, i_pre, f_pre):
    B, S, H, d = q.shape
    L = ML_CHUNK
    nc = S // L

    def chunks4(a):
        return a.astype(jnp.float32).reshape(B, nc, L, H, d).transpose(1, 0, 3, 2, 4)

    def chunks3(a):
        return a.astype(jnp.float32).reshape(B, nc, L, H).transpose(1, 0, 3, 2)

    log_f = jax.nn.log_sigmoid(f_pre.astype(jnp.float32))
    tril = jnp.tril(jnp.ones((L, L), dtype=bool))

    def step(carry, xs):
        C, n, m = carry
        qc, kc, vc, ic, lfc = xs
        b = jnp.cumsum(lfc, axis=-1)
        dmat = b[..., :, None] - b[..., None, :] + ic[..., None, :]
        dmat = jnp.where(tril, dmat, -jnp.inf)
        inter = b + m[..., None]
        m_row = jnp.maximum(jnp.max(dmat, axis=-1), inter)
        w_intra = jnp.exp(dmat - m_row[..., None])
        w_inter = jnp.exp(inter - m_row)
        s = jnp.einsum('bhld,bhsd->bhls', qc, kc) * w_intra
        num = w_inter[..., None] * jnp.einsum('bhld,bhde->bhle', qc, C) + jnp.einsum('bhls,bhse->bhle', s, vc)
        den = w_inter * jnp.einsum('bhld,bhd->bhl', qc, n) + jnp.sum(s, axis=-1)
        h = num / jnp.maximum(jnp.abs(den), jnp.exp(-m_row))[..., None]
        b_last = b[..., -1]
        g = b_last[..., None] - b + ic
        m_new = jnp.maximum(b_last + m, jnp.max(g, axis=-1))
        w_k = jnp.exp(g - m_new[..., None])
        decay = jnp.exp(b_last + m - m_new)
        C = decay[..., None, None] * C + jnp.einsum('bhs,bhsd,bhse->bhde', w_k, kc, vc)
        n = decay[..., None] * n + jnp.einsum('bhs,bhsd->bhd', w_k, kc)
        return (C, n, m_new), h

    init = (jnp.zeros((B, H, d, d), jnp.float32), jnp.zeros((B, H, d), jnp.float32), jnp.zeros((B, H), jnp.float32))
    _, hs = lax.scan(step, init, (chunks4(q), chunks4(k), chunks4(v), chunks3(i_pre), chunks3(log_f)))
    return hs.transpose(1, 0, 3, 2, 4).reshape(B, S, H, d)


def head_norm(h, g):
    mu = jnp.mean(h, axis=-1, keepdims=True)
    var = jnp.mean(jnp.square(h - mu), axis=-1, keepdims=True)
    return (h - mu) * lax.rsqrt(var + LN_EPS) * g.astype(jnp.float32).reshape(ML_HEADS, ML_HEAD_DIM)


def setup_inputs(seed: int = 0) -> dict:
    key = jax.random.key(seed)
    ks = jax.random.split(key, 24)
    f32 = jnp.float32
    nrm = lambda k, shape, scale: jax.random.normal(k, shape, f32) * scale
    offsets = jax.random.randint(ks[2], (BATCH,), 0, 4096, dtype=jnp.int32)
    positions = offsets[:, None] + jnp.arange(SEQ, dtype=jnp.int32)[None, :]
    b_f = jnp.broadcast_to(jnp.linspace(3.0, 6.0, ML_HEADS, dtype=f32), (DEPTH, ML_HEADS))
    return {
        'x': nrm(ks[0], (BATCH, SEQ, D_MODEL), 1.0),
        'p': nrm(ks[1], (DEPTH, BATCH, SEQ, PLE_DIM), 1.0),
        'positions': positions,
        'w_in': nrm(ks[3], (DEPTH, D_MODEL, W_IN_COLS), D_MODEL ** -0.5),
        'conv_w': nrm(ks[4], (DEPTH, CONV_WIDTH, 2 * ML_W), CONV_WIDTH ** -0.5),
        'conv_b': nrm(ks[5], (DEPTH, 2 * ML_W), 0.01),
        'b_igate': nrm(ks[6], (DEPTH, ML_HEADS), 0.1),
        'b_fgate': b_f + nrm(ks[7], (DEPTH, ML_HEADS), 0.1),
        'ml_norm_g': 1.0 + nrm(ks[8], (DEPTH, ML_W), 0.02),
        'w_up_a': nrm(ks[9], (DEPTH, ATT_W, D_MODEL), ATT_W ** -0.5),
        'w_up_b': nrm(ks[10], (DEPTH, ML_W, D_MODEL), ML_W ** -0.5),
        'w_out': nrm(ks[11], (DEPTH, D_MODEL, D_MODEL), D_MODEL ** -0.5 * DEEPNORM_BETA),
        'ln1_g': 1.0 + nrm(ks[12], (DEPTH, D_MODEL), 0.02),
        'ln1_b': nrm(ks[13], (DEPTH, D_MODEL), 0.02),
        'w_ff1': nrm(ks[14], (DEPTH, D_MODEL, D_FF), D_MODEL ** -0.5),
        'w_ff2': nrm(ks[15], (DEPTH, D_FF, D_MODEL), D_FF ** -0.5 * DEEPNORM_BETA),
        'w_ple_gate': nrm(ks[16], (DEPTH, D_MODEL, D_MODEL), D_MODEL ** -0.5),
        'w_ple_proj': nrm(ks[17], (DEPTH, PLE_DIM, D_MODEL), PLE_DIM ** -0.5 * DEEPNORM_BETA),
        'ln2_g': 1.0 + nrm(ks[18], (DEPTH, D_MODEL), 0.02),
        'ln2_b': nrm(ks[19], (DEPTH, D_MODEL), 0.02),
    }


def reference(x, p, positions, w_in, conv_w, conv_b, b_igate, b_fgate, ml_norm_g, w_up_a, w_up_b, w_out,
              ln1_g, ln1_b, w_ff1, w_ff2, w_ple_gate, w_ple_proj, ln2_g, ln2_b):
    B, S, _ = x.shape
    cos_a, sin_a = rope_tables(positions, ATT_HEAD_DIM)
    cos_i, sin_i = rope_tables(positions, IDX_DIM)
    h = x
    for l in range(DEPTH):
        parts = dict(zip(SPLIT_NAMES, jnp.split(h @ w_in[l], SPLIT_OFFSETS, axis=-1)))
        q_a = apply_rope(parts['att_q'].reshape(B, S, ATT_HEADS, ATT_HEAD_DIM), cos_a, sin_a)
        k_a = apply_rope(parts['att_k'][:, :, None, :], cos_a, sin_a)[:, :, 0]
        q_i = apply_rope(parts['idx_q'].reshape(B, S, IDX_HEADS, IDX_DIM), cos_i, sin_i)
        k_i = apply_rope(parts['idx_k'][:, :, None, :], cos_i, sin_i)[:, :, 0]
        y_a = dsa_attention(q_a, k_a, parts['att_v'], q_i, k_i, parts['idx_w'])
        qk = jax.nn.silu(causal_dwconv(jnp.concatenate([parts['ml_q'], parts['ml_k']], axis=-1), conv_w[l], conv_b[l]))
        mq, mk = jnp.split(qk, 2, axis=-1)
        mq = mq.reshape(B, S, ML_HEADS, ML_HEAD_DIM)
        mk = mk.reshape(B, S, ML_HEADS, ML_HEAD_DIM) * (ML_HEAD_DIM ** -0.5)
        mv = parts['ml_v'].reshape(B, S, ML_HEADS, ML_HEAD_DIM)
        hm = mlstm(mq, mk, mv, parts['ml_i'] + b_igate[l], parts['ml_f'] + b_fgate[l])
        hm = head_norm(hm, ml_norm_g[l]).reshape(B, S, ML_W).astype(x.dtype)
        y_b = jax.nn.sigmoid(parts['ml_o']) * hm
        merged = jax.nn.sigmoid(parts['gate_a']) * (y_a @ w_up_a[l]) + jax.nn.sigmoid(parts['gate_b']) * (y_b @ w_up_b[l])
        h = layer_norm(DEEPNORM_ALPHA * h + merged @ w_out[l], ln1_g[l], ln1_b[l])
        ff = jnp.square(jax.nn.relu(h @ w_ff1[l])) @ w_ff2[l]
        r = DEEPNORM_ALPHA * h + ff
        r = r + jax.nn.sigmoid(r @ w_ple_gate[l]) * (p[l] @ w_ple_proj[l])
        h = layer_norm(r, ln2_g[l], ln2_b[l])
    return h
```

```python
import functools

import jax
import jax.numpy as jnp
import numpy as np
from jax import lax
from jax.experimental import pallas as pl
from jax.experimental.pallas import tpu as pltpu

F32 = jnp.float32
BF16 = jnp.bfloat16
I32 = jnp.int32

D_MODEL = 1024
ATT_HEADS = 8
ATT_HEAD_DIM = 64
IDX_HEADS = 8
IDX_DIM = 64
TOPK_MAX = 256
ML_HEADS = 4
ML_HEAD_DIM = 128
CONV_WIDTH = 4
D_FF = 4 * D_MODEL
PLE_DIM = 256
ROPE_THETA = 10000.0
LN_EPS = 1e-5
DEPTH = 1
DEEPNORM_ALPHA = (2.0 * DEPTH) ** 0.25
IDX_W_SCALE = (IDX_HEADS ** -0.5) * (IDX_DIM ** -0.5)

ATT_W = ATT_HEADS * ATT_HEAD_DIM
IDX_QW = IDX_HEADS * IDX_DIM
ML_W = ML_HEADS * ML_HEAD_DIM
assert ATT_HEAD_DIM == IDX_DIM == 64 and ATT_HEADS == IDX_HEADS == 8

_SPLIT = (
    ('att_q', ATT_W), ('att_k', ATT_HEAD_DIM), ('att_v', ATT_HEAD_DIM),
    ('idx_q', IDX_QW), ('idx_k', IDX_DIM), ('idx_w', IDX_HEADS),
    ('ml_q', ML_W), ('ml_k', ML_W), ('ml_v', ML_W),
    ('ml_i', ML_HEADS), ('ml_f', ML_HEADS), ('ml_o', ML_W),
    ('gate_a', D_MODEL), ('gate_b', D_MODEL),
)

LANES = 128
VMEM_LIMIT = 56 * 1024 * 1024

TM = 512
TQ = 128
KC = 512
N_BISECT = 18
ML_L = 128
INT_MIN = -2 ** 31
INT_MAX = 2 ** 31 - 1


def _const_spec(shape):
    nd = len(shape)
    return pl.BlockSpec(shape, lambda *_: (0,) * nd, pipeline_mode=pl.Buffered(1))


def _bdot(a, b):
    return jnp.dot(a, b, preferred_element_type=F32)


def _dot_nt(a, b):
    return lax.dot_general(a, b, (((1,), (1,)), ((), ())), preferred_element_type=F32)


def _dot_tn(a, b):
    return lax.dot_general(a, b, (((0,), (0,)), ((), ())), preferred_element_type=F32)


def _layer_norm(r, g, b):
    mu = jnp.mean(r, axis=-1, keepdims=True)
    d = r - mu
    var = jnp.mean(d * d, axis=-1, keepdims=True)
    return d * lax.rsqrt(var + LN_EPS) * g + b


def _proj_kernel(tiles_per_seq, x_ref, xh_ref, pos_ref, invf_ref, wn_ref, wt_ref, cw_ref, cb_ref, gb_ref,
                 qa_ref, qi_ref, kk_ref, mq_ref, mk_ref, mv_ref, og_ref, vt_ref, gt_ref):
    i = pl.program_id(0)
    xb = x_ref[...].astype(BF16)
    ang = pos_ref[...].astype(F32) * invf_ref[...]
    lane = lax.broadcasted_iota(I32, (1, LANES), 1)
    first_half = (lane % ATT_HEAD_DIM) < (ATT_HEAD_DIM // 2)
    cos = jnp.cos(ang)
    sin = jnp.where(first_half, -jnp.sin(ang), jnp.sin(ang))

    def rope(v):
        swapped = jnp.where(first_half, pltpu.roll(v, LANES - 32, 1), pltpu.roll(v, 32, 1))
        return v * cos + swapped * sin

    def rope_wide(off, out_ref):
        for g in range(ATT_W // LANES):
            v = _bdot(xb, wn_ref[:, off + g * LANES: off + (g + 1) * LANES])
            out_ref[:, g * LANES:(g + 1) * LANES] = rope(v).astype(out_ref.dtype)

    rope_wide(0, qa_ref)
    rope_wide(ATT_W, qi_ref)
    kk_ref[...] = rope(_bdot(xb, wn_ref[:, 2 * ATT_W: 2 * ATT_W + LANES])).astype(kk_ref.dtype)

    c0 = 2 * ATT_W + LANES
    xh = xh_ref[...].astype(BF16)
    not_first = (i % tiles_per_seq != 0).astype(F32)
    for g in range(2 * ML_W // LANES // 2):
        cs = slice(c0 + g * 256, c0 + (g + 1) * 256)
        ls = slice(g * 256, (g + 1) * 256)
        pre = _bdot(xb, wn_ref[:, cs])
        halo = _bdot(xh, wn_ref[:, cs]) * not_first
        ext = jnp.concatenate([halo, pre], axis=0)
        y = cb_ref[:, ls] + cw_ref[CONV_WIDTH - 1: CONV_WIDTH, ls] * pre
        for k in range(1, CONV_WIDTH):
            sh = pltpu.roll(ext, k, 0)[8:, :]
            y = y + cw_ref[CONV_WIDTH - 1 - k: CONV_WIDTH - k, ls] * sh
        y = y * jax.nn.sigmoid(y)
        if g < 2:
            mq_ref[:, g * 256:(g + 1) * 256] = y.astype(mq_ref.dtype)
        else:
            mk_ref[:, (g - 2) * 256:(g - 1) * 256] = (y * (ML_HEAD_DIM ** -0.5)).astype(mk_ref.dtype)
    c1 = c0 + 2 * ML_W
    mv_ref[...] = _bdot(xb, wn_ref[:, c1: c1 + ML_W]).astype(mv_ref.dtype)
    og_ref[...] = jax.nn.sigmoid(_bdot(xb, wn_ref[:, c1 + ML_W: c1 + 2 * ML_W])).astype(og_ref.dtype)

    t = _dot_nt(wt_ref[...], xb)
    vt_ref[0] = t[:ATT_HEAD_DIM].astype(vt_ref.dtype)
    g = t[ATT_HEAD_DIM:] + gb_ref[...]
    row = lax.broadcasted_iota(I32, (16, 1), 0)
    gt_ref[0] = jnp.where(row < IDX_HEADS, g * IDX_W_SCALE,
                          jnp.where(row < IDX_HEADS + ML_HEADS, g,
                                    jnp.minimum(g, 0.0) - jnp.log1p(jnp.exp(-jnp.abs(g)))))


def _proj_call(x2, pos2, invf, wn, wt, cw, cb, gb, B, S):
    N = x2.shape[0]
    tps = S // TM
    ncol = wn.shape[1]
    row = lambda w: pl.BlockSpec((TM, w), lambda i: (i, 0))
    out_shape = [jax.ShapeDtypeStruct((N, ATT_W), BF16), jax.ShapeDtypeStruct((N, IDX_QW), BF16),
                 jax.ShapeDtypeStruct((N, LANES), BF16)] + [jax.ShapeDtypeStruct((N, ML_W), BF16)] * 4 + [
        jax.ShapeDtypeStruct((B, ATT_HEAD_DIM, S), BF16), jax.ShapeDtypeStruct((B, 16, S), F32)]
    out_specs = [row(ATT_W), row(IDX_QW), row(LANES)] + [row(ML_W)] * 4 + [
        pl.BlockSpec((1, ATT_HEAD_DIM, TM), lambda i: (i // tps, 0, i % tps)),
        pl.BlockSpec((1, 16, TM), lambda i: (i // tps, 0, i % tps))]
    return pl.pallas_call(
        functools.partial(_proj_kernel, tps),
        grid=(N // TM,),
        in_specs=[row(D_MODEL),
                  pl.BlockSpec((8, D_MODEL), lambda i: (jnp.maximum(i * (TM // 8) - 1, 0), 0)),
                  row(1), _const_spec((1, LANES)), _const_spec((D_MODEL, ncol)), _const_spec(wt.shape),
                  _const_spec(cw.shape), _const_spec(cb.shape), _const_spec(gb.shape)],
        out_specs=out_specs, out_shape=out_shape,
        compiler_params=pltpu.CompilerParams(dimension_semantics=("arbitrary",), vmem_limit_bytes=VMEM_LIMIT),
        name="proj",
    )(x2, x2, pos2, invf, wn, wt, cw, cb, gb)


def _stack_heads_t(q_ref, low_rows):
    qt = q_ref[0].astype(F32).T
    z = jnp.zeros((ATT_HEAD_DIM, TQ), F32)
    cols = []
    for h in range(ATT_HEADS):
        blk = qt[h * ATT_HEAD_DIM:(h + 1) * ATT_HEAD_DIM, :]
        cols.append(jnp.concatenate([blk, z] if low_rows else [z, blk], axis=0))
    return jnp.concatenate(cols, axis=1).astype(BF16)


def _attn_kernel(k_sel, idx_bits, qa_ref, qi_ref, kk_ref, vt_ref, gq_ref, o_ref, sc_ref, s_ref):
    j = pl.program_id(1)
    nch = (j * TQ + TQ + KC - 1) // KC
    qpos = j * TQ + lax.broadcasted_iota(I32, (1, TQ), 1)
    kpos0 = lax.broadcasted_iota(I32, (KC, TQ), 0)
    qi_s = _stack_heads_t(qi_ref, False)
    wq = gq_ref[0, 0:IDX_HEADS, :]

    def fold(a, op):
        parts = [a[r * 8:(r + 1) * 8] for r in range(a.shape[0] // 8)]
        while len(parts) > 1:
            parts = [op(parts[i], parts[i + 1]) for i in range(0, len(parts), 2)]
        return parts[0]

    def phase_a(c, carry):
        smin, smax = carry
        k0 = pl.multiple_of(c * KC, KC)
        lg = _bdot(kk_ref[0, pl.ds(k0, KC), :], qi_s)
        sc = jnp.zeros((KC, TQ), F32)
        for h in range(IDX_HEADS):
            sc = sc + jnp.maximum(lg[:, h * TQ:(h + 1) * TQ], 0.0) * wq[h:h + 1, :]
        causal = kpos0 + k0 <= qpos
        sc_ref[pl.ds(k0, KC), :] = jnp.where(causal, sc, -jnp.inf)
        smin = jnp.minimum(smin, fold(jnp.where(causal, sc, jnp.inf), jnp.minimum))
        smax = jnp.maximum(smax, fold(jnp.where(causal, sc, -jnp.inf), jnp.maximum))
        return smin, smax

    smin, smax = lax.fori_loop(0, nch, phase_a, (jnp.full((8, TQ), jnp.inf, F32), jnp.full((8, TQ), -jnp.inf, F32)))

    def count(pred):
        def body(c, acc):
            k0 = pl.multiple_of(c * KC, KC)
            return acc + fold(pred(sc_ref[pl.ds(k0, KC), :], k0).astype(I32), jnp.add)
        acc = lax.fori_loop(0, nch, body, jnp.zeros((8, TQ), I32))
        return jnp.sum(acc, axis=0, keepdims=True)

    def max_below(bound):
        def body(c, acc):
            s = sc_ref[pl.ds(pl.multiple_of(c * KC, KC), KC), :]
            return jnp.maximum(acc, fold(jnp.where(s < bound, s, -jnp.inf), jnp.maximum))
        acc = lax.fori_loop(0, nch, body, jnp.full((8, TQ), -jnp.inf, F32))
        return jnp.max(acc, axis=0, keepdims=True)

    kq = jnp.minimum(qpos + 1, k_sel)

    def bisect(_, st):
        lo, hi, hi_ok, clo = st
        x = 0.5 * lo + 0.5 * hi
        c = count(lambda s, k0: s >= x)
        ge = c >= kq
        return jnp.where(ge, x, lo), jnp.where(ge, hi, x), jnp.where(ge, hi_ok, 1), jnp.where(ge, c, clo)

    st = (jnp.min(smin, axis=0, keepdims=True), jnp.max(smax, axis=0, keepdims=True),
          jnp.zeros((1, TQ), I32), qpos + 1)
    st = lax.fori_loop(0, N_BISECT, bisect, st)

    def exact_check(st, tie):
        lo, hi, hi_ok, clo = st
        rem = (clo != kq) & (tie == 0)
        v = max_below(jnp.where(hi_ok > 0, hi, jnp.inf))
        cv = count(lambda s, k0: s >= v)
        hit = rem & (cv >= kq)
        miss = rem & (cv < kq)
        st = (jnp.where(hit, v, lo), jnp.where(miss, v, hi), jnp.where(miss, 1, hi_ok), jnp.where(hit, kq, clo))
        return st, jnp.where(hit & (cv > kq), 1, tie)

    def flags(st, tie):
        return jnp.max(jnp.where((st[3] != kq) & (tie == 0), 2, 0) + tie)

    st, tie = exact_check(st, jnp.zeros((1, TQ), I32))

    def refine(carry):
        st, tie, _ = carry
        st, tie = exact_check(lax.fori_loop(0, 2, bisect, st), tie)
        return st, tie, flags(st, tie)

    st, tie, code = lax.while_loop(lambda c: c[2] >= 2, refine, (st, tie, flags(st, tie)))
    thr = st[0]

    def tie_break(_):
        need = kq - count(lambda s, k0: s > thr)
        def step(it, p):
            cand = p | jnp.left_shift(jnp.int32(1), idx_bits - 1 - it)
            cnt = count(lambda s, k0: (s == thr) & (kpos0 + k0 < cand))
            return jnp.where(cnt < need, cand, p)
        return lax.fori_loop(0, idx_bits, step, jnp.zeros((1, TQ), I32))

    plim = lax.cond(code > 0, tie_break, lambda _: jnp.full((1, TQ), INT_MAX, I32), 0)

    qa_s = _stack_heads_t(qa_ref, True)
    W = ATT_HEADS * TQ

    def logits(c, mx):
        k0 = pl.multiple_of(c * KC, KC)
        sc = sc_ref[pl.ds(k0, KC), :]
        sel = (sc > thr) | ((sc == thr) & (kpos0 + k0 <= plim))
        bias = jnp.where(sel, 0.0, -jnp.inf)
        s = _bdot(kk_ref[0, pl.ds(k0, KC), :], qa_s)
        cols = []
        for h in range(ATT_HEADS):
            sh = s[:, h * TQ:(h + 1) * TQ] + bias
            s_ref[pl.ds(k0, KC), h * TQ:(h + 1) * TQ] = sh
            cols.append(fold(sh, jnp.maximum))
        return jnp.maximum(mx, jnp.concatenate(cols, axis=1))

    mx = lax.fori_loop(0, nch, logits, jnp.full((8, W), -jnp.inf, F32))
    mx = jnp.max(mx, axis=0, keepdims=True)

    def weighted(c, carry):
        l, acc = carry
        k0 = pl.multiple_of(c * KC, KC)
        cols, ps = [], []
        for h in range(ATT_HEADS):
            hs = slice(h * TQ, (h + 1) * TQ)
            p = jnp.exp(s_ref[pl.ds(k0, KC), hs] - mx[:, hs])
            cols.append(fold(p, jnp.add))
            ps.append(p.astype(BF16))
        acc = acc + _bdot(vt_ref[0, :, pl.ds(k0, KC)], jnp.concatenate(ps, axis=1))
        return l + jnp.concatenate(cols, axis=1), acc

    l, acc = lax.fori_loop(0, nch, weighted, (jnp.zeros((8, W), F32), jnp.zeros((ATT_HEAD_DIM, W), F32)))
    out_t = acc / jnp.sum(l, axis=0, keepdims=True)
    out_t = jnp.concatenate([out_t[:, h * TQ:(h + 1) * TQ] for h in range(ATT_HEADS)], axis=0)
    o_ref[0] = out_t.T.astype(o_ref.dtype)


def _attn_call(qa, qi, kk, vt, gt, B, S):
    k_sel = min(TOPK_MAX, S // 4)
    qspec = pl.BlockSpec((1, TQ, ATT_W), lambda b, j: (b, j, 0))
    return pl.pallas_call(
        functools.partial(_attn_kernel, k_sel, (S - 1).bit_length()),
        grid=(B, S // TQ),
        in_specs=[qspec, qspec,
                  pl.BlockSpec((1, S, LANES), lambda b, j: (b, 0, 0)),
                  pl.BlockSpec((1, ATT_HEAD_DIM, S), lambda b, j: (b, 0, 0)),
                  pl.BlockSpec((1, 16, TQ), lambda b, j: (b, 0, j))],
        out_specs=qspec,
        out_shape=jax.ShapeDtypeStruct((B, S, ATT_W), BF16),
        scratch_shapes=[pltpu.VMEM((S, TQ), F32), pltpu.VMEM((S, ATT_HEADS * TQ), F32)],
        compiler_params=pltpu.CompilerParams(dimension_semantics=("arbitrary", "arbitrary"),
                                             vmem_limit_bytes=VMEM_LIMIT),
        name="attn",
    )(qa, qi, kk, vt, gt)


def _lane_scan(x, op, identity):
    lane = lax.broadcasted_iota(I32, x.shape, 1)
    s = 1
    while s < x.shape[1]:
        x = op(x, jnp.where(lane >= s, pltpu.roll(x, s, 1), identity))
        s *= 2
    return x


def _mlstm_kernel(nchunks, mq_ref, mk_ref, mv_ref, og_ref, gt_ref, ng_ref, y_ref, c_ref):
    L, H, Dh = ML_L, ML_HEADS, ML_HEAD_DIM
    c_ref[...] = jnp.zeros_like(c_ref)
    r_i = lax.broadcasted_iota(I32, (L, L), 0)
    c_i = lax.broadcasted_iota(I32, (L, L), 1)
    eye = r_i == c_i
    tril = r_i >= c_i
    ones = jnp.ones((L, LANES), BF16)

    def to_col(r):
        return jnp.sum(jnp.where(eye, r, 0.0), axis=1, keepdims=True)

    def chunk(c, m_all):
        t0 = pl.multiple_of(c * L, L)
        gi = gt_ref[0, IDX_HEADS:IDX_HEADS + H, pl.ds(t0, L)]
        lf = gt_ref[0, IDX_HEADS + H:IDX_HEADS + 2 * H, pl.ds(t0, L)]
        b_all = _lane_scan(lf, jnp.add, 0.0)
        a_all = gi - b_all
        cm_all = _lane_scan(a_all, jnp.maximum, -jnp.inf)
        m_out = []
        for h in range(H):
            hs = slice(h * Dh, (h + 1) * Dh)
            m = m_all[h]
            b_row, a_row = b_all[h:h + 1], a_all[h:h + 1]
            mrow = jnp.maximum(cm_all[h:h + 1], m)
            m_last = jnp.max(mrow, axis=1, keepdims=True)
            b_last = b_row[:, L - 1:L]
            b_col, a_col, mcol = to_col(b_row), to_col(a_row), to_col(mrow)
            q = mq_ref[0, pl.ds(t0, L), hs]
            k = mk_ref[0, pl.ds(t0, L), hs]
            v_aug = jnp.concatenate([mv_ref[0, pl.ds(t0, L), hs], ones], axis=1)
            w_intra = jnp.where(tril, jnp.exp(a_row - mcol), 0.0)
            s = (_dot_nt(q, k) * w_intra).astype(BF16)
            cst = c_ref[h]
            haug = jnp.exp(m - mcol) * _bdot(q, cst.astype(BF16)) + _bdot(s, v_aug)
            num, den = haug[:, :Dh], haug[:, Dh:]
            hm = num / jnp.maximum(jnp.abs(den), jnp.exp(-(b_col + mcol)))
            wv = (v_aug.astype(F32) * jnp.exp(a_col - m_last)).astype(BF16)
            c_ref[h] = jnp.exp(m - m_last) * cst + _dot_tn(k, wv)
            m_out.append(b_last + m_last)
            mu = jnp.mean(hm, axis=-1, keepdims=True)
            d = hm - mu
            var = jnp.mean(d * d, axis=-1, keepdims=True)
            yn = d * lax.rsqrt(var + LN_EPS) * ng_ref[:, hs]
            y_ref[0, pl.ds(t0, L), hs] = (og_ref[0, pl.ds(t0, L), hs].astype(F32) * yn).astype(y_ref.dtype)
        return tuple(m_out)

    lax.fori_loop(0, nchunks, chunk, tuple(jnp.zeros((1, 1), F32) for _ in range(H)))


def _mlstm_call(mq, mk, mv, og, gt, ng, B, S):
    seq = pl.BlockSpec((1, S, ML_W), lambda b: (b, 0, 0))
    return pl.pallas_call(
        functools.partial(_mlstm_kernel, S // ML_L),
        grid=(B,),
        in_specs=[seq, seq, seq, seq, pl.BlockSpec((1, 16, S), lambda b: (b, 0, 0)), _const_spec(ng.shape)],
        out_specs=seq,
        out_shape=jax.ShapeDtypeStruct((B, S, ML_W), BF16),
        scratch_shapes=[pltpu.VMEM((ML_HEADS, ML_HEAD_DIM, 2 * ML_HEAD_DIM), F32)],
        compiler_params=pltpu.CompilerParams(dimension_semantics=("arbitrary",), vmem_limit_bytes=VMEM_LIMIT),
        name="mlstm",
    )(mq, mk, mv, og, gt, ng)


def _merge_kernel(x_ref, ya_ref, yb_ref, wg_ref, wua_ref, wub_ref, wo_ref, g_ref, b_ref, h_ref):
    x = x_ref[...]
    xb = x.astype(BF16)
    ma = jax.nn.sigmoid(_bdot(xb, wg_ref[:, :D_MODEL])) * _bdot(ya_ref[...], wua_ref[...])
    mb = jax.nn.sigmoid(_bdot(xb, wg_ref[:, D_MODEL:])) * _bdot(yb_ref[...], wub_ref[...])
    r = DEEPNORM_ALPHA * x + _bdot((ma + mb).astype(BF16), wo_ref[...])
    h_ref[...] = _layer_norm(r, g_ref[...], b_ref[...])


def _merge_call(x2, ya, yb, wg, wua, wub, wo, g, b):
    N = x2.shape[0]
    row = lambda w: pl.BlockSpec((TM, w), lambda i: (i, 0))
    return pl.pallas_call(
        _merge_kernel,
        grid=(N // TM,),
        in_specs=[row(D_MODEL), row(ATT_W), row(ML_W)] + [_const_spec(a.shape) for a in (wg, wua, wub, wo, g, b)],
        out_specs=row(D_MODEL),
        out_shape=jax.ShapeDtypeStruct((N, D_MODEL), F32),
        compiler_params=pltpu.CompilerParams(dimension_semantics=("arbitrary",), vmem_limit_bytes=VMEM_LIMIT),
        name="merge",
    )(x2, ya, yb, wg, wua, wub, wo, g, b)


def _ffn_kernel(h_ref, p_ref, w1_ref, w2_ref, wpg_ref, wpp_ref, g_ref, b_ref, o_ref):
    h = h_ref[...]
    hb = h.astype(BF16)
    ff = jnp.zeros_like(h)
    for c in range(D_FF // D_MODEL):
        cs = slice(c * D_MODEL, (c + 1) * D_MODEL)
        a = jnp.maximum(_bdot(hb, w1_ref[:, cs]), 0.0)
        ff = ff + _bdot((a * a).astype(BF16), w2_ref[cs, :])
    r = DEEPNORM_ALPHA * h + ff
    r = r + jax.nn.sigmoid(_bdot(r.astype(BF16), wpg_ref[...])) * _bdot(p_ref[...].astype(BF16), wpp_ref[...])
    o_ref[...] = _layer_norm(r, g_ref[...], b_ref[...])


def _ffn_call(h1, p2, w1, w2, wpg, wpp, g, b):
    N = h1.shape[0]
    row = lambda w: pl.BlockSpec((TM, w), lambda i: (i, 0))
    return pl.pallas_call(
        _ffn_kernel,
        grid=(N // TM,),
        in_specs=[row(D_MODEL), row(PLE_DIM)] + [_const_spec(a.shape) for a in (w1, w2, wpg, wpp, g, b)],
        out_specs=row(D_MODEL),
        out_shape=jax.ShapeDtypeStruct((N, D_MODEL), F32),
        compiler_params=pltpu.CompilerParams(dimension_semantics=("arbitrary",), vmem_limit_bytes=VMEM_LIMIT),
        name="ffn",
    )(h1, p2, w1, w2, wpg, wpp, g, b)


def _split_w_in(w):
    parts, off = {}, 0
    for name, width in _SPLIT:
        parts[name] = w[:, off:off + width]
        off += width
    return parts


def kernel(x, p, positions, w_in, conv_w, conv_b, b_igate, b_fgate, ml_norm_g, w_up_a, w_up_b, w_out,
           ln1_g, ln1_b, w_ff1, w_ff2, w_ple_gate, w_ple_proj, ln2_g, ln2_b):
    B, S, D = x.shape
    assert D == D_MODEL and S % TM == 0 and S % KC == 0 and w_in.shape[0] == DEPTH == 1
    N = B * S
    x2 = x.reshape(N, D)
    pos2 = positions.reshape(N, 1)
    half = ATT_HEAD_DIM // 2
    inv_freq = 1.0 / (ROPE_THETA ** (jnp.arange(0, ATT_HEAD_DIM, 2, dtype=F32) / ATT_HEAD_DIM))
    invf = jnp.tile(inv_freq, LANES // half).reshape(1, LANES)

    w = _split_w_in(w_in[0])
    wn = jnp.concatenate([w['att_q'] * (ATT_HEAD_DIM ** -0.5), w['idx_q'], w['att_k'], w['idx_k'],
                          w['ml_q'], w['ml_k'], w['ml_v'], w['ml_o']], axis=1).astype(BF16)
    wt = jnp.concatenate([w['att_v'], w['idx_w'], w['ml_i'], w['ml_f']], axis=1).T.astype(BF16)
    gb = jnp.concatenate([jnp.zeros((IDX_HEADS,), F32), b_igate[0], b_fgate[0]]).reshape(16, 1)
    wg = jnp.concatenate([w['gate_a'], w['gate_b']], axis=1).astype(BF16)

    qa, qi, kk, mq, mk, mv, og, vt, gt = _proj_call(x2, pos2, invf, wn, wt, conv_w[0], conv_b[0].reshape(1, -1),
                                                    gb, B, S)
    r3 = lambda a: a.reshape(B, S, a.shape[-1])
    ya = _attn_call(r3(qa), r3(qi), r3(kk), vt, gt, B, S)
    yb = _mlstm_call(r3(mq), r3(mk), r3(mv), r3(og), gt, ml_norm_g[0].reshape(1, -1), B, S)
    h1 = _merge_call(x2, ya.reshape(N, ATT_W), yb.reshape(N, ML_W), wg, w_up_a[0].astype(BF16),
                     w_up_b[0].astype(BF16), w_out[0].astype(BF16), ln1_g[0].reshape(1, -1), ln1_b[0].reshape(1, -1))
    out = _ffn_call(h1, p[0].reshape(N, PLE_DIM), w_ff1[0].astype(BF16), w_ff2[0].astype(BF16),
                    w_ple_gate[0].astype(BF16), w_ple_proj[0].astype(BF16),
                    ln2_g[0].reshape(1, -1), ln2_b[0].reshape(1, -1))
    return out.reshape(B, S, D)
```

```python
import functools

import jax
import jax.numpy as jnp
import numpy as np
from jax import lax
from jax.experimental import pallas as pl
from jax.experimental.pallas import tpu as pltpu

F32 = jnp.float32
BF16 = jnp.bfloat16
I32 = jnp.int32

D_MODEL = 1024
ATT_HEADS = 8
ATT_HEAD_DIM = 64
IDX_HEADS = 8
IDX_DIM = 64
TOPK_MAX = 256
ML_HEADS = 4
ML_HEAD_DIM = 128
CONV_WIDTH = 4
D_FF = 4 * D_MODEL
PLE_DIM = 256
ROPE_THETA = 10000.0
LN_EPS = 1e-5
DEPTH = 1
DEEPNORM_ALPHA = (2.0 * DEPTH) ** 0.25
IDX_W_SCALE = (IDX_HEADS ** -0.5) * (IDX_DIM ** -0.5)
LOG2E = 1.4426950408889634

ATT_W = ATT_HEADS * ATT_HEAD_DIM
IDX_QW = IDX_HEADS * IDX_DIM
ML_W = ML_HEADS * ML_HEAD_DIM
assert ATT_HEAD_DIM == IDX_DIM == 64 and ATT_HEADS == IDX_HEADS == 8

_SPLIT = (
    ('att_q', ATT_W), ('att_k', ATT_HEAD_DIM), ('att_v', ATT_HEAD_DIM),
    ('idx_q', IDX_QW), ('idx_k', IDX_DIM), ('idx_w', IDX_HEADS),
    ('ml_q', ML_W), ('ml_k', ML_W), ('ml_v', ML_W),
    ('ml_i', ML_HEADS), ('ml_f', ML_HEADS), ('ml_o', ML_W),
    ('gate_a', D_MODEL), ('gate_b', D_MODEL),
)

LANES = 128
VMEM_LIMIT = 56 * 1024 * 1024

TM = 512
TQ = 128
KC = 512
N_BISECT = 15
ML_L = 128
INT_MIN = -2 ** 31
INT_MAX = 2 ** 31 - 1


def _const_spec(shape):
    nd = len(shape)
    return pl.BlockSpec(shape, lambda *_: (0,) * nd, pipeline_mode=pl.Buffered(1))


def _bdot(a, b):
    return jnp.dot(a, b, preferred_element_type=F32)


def _dot_nt(a, b):
    return lax.dot_general(a, b, (((1,), (1,)), ((), ())), preferred_element_type=F32)


def _dot_tn(a, b):
    return lax.dot_general(a, b, (((0,), (0,)), ((), ())), preferred_element_type=F32)


def _layer_norm(r, g, b):
    mu = jnp.mean(r, axis=-1, keepdims=True)
    d = r - mu
    var = jnp.mean(d * d, axis=-1, keepdims=True)
    return d * lax.rsqrt(var + LN_EPS) * g + b


def _proj_kernel(tiles_per_seq, x_ref, xh_ref, pos_ref, invf_ref, wn_ref, wt_ref, cw_ref, cb_ref, gb_ref,
                 qa_ref, qi_ref, kk_ref, mq_ref, mk_ref, mv_ref, og_ref, vt_ref, gt_ref, ext_ref):
    i = pl.program_id(0)
    xb = x_ref[...].astype(BF16)
    ang = pos_ref[...].astype(F32) * invf_ref[...]
    lane = lax.broadcasted_iota(I32, (1, LANES), 1)
    first_half = (lane % ATT_HEAD_DIM) < (ATT_HEAD_DIM // 2)
    cos = jnp.cos(ang)
    sin = jnp.where(first_half, -jnp.sin(ang), jnp.sin(ang))

    def rope(v):
        swapped = jnp.where(first_half, pltpu.roll(v, LANES - 32, 1), pltpu.roll(v, 32, 1))
        return v * cos + swapped * sin

    def rope_wide(off, out_ref):
        for g in range(ATT_W // LANES):
            v = _bdot(xb, wn_ref[:, off + g * LANES: off + (g + 1) * LANES])
            out_ref[:, g * LANES:(g + 1) * LANES] = rope(v).astype(out_ref.dtype)

    rope_wide(0, qa_ref)
    rope_wide(ATT_W, qi_ref)
    kk_ref[...] = rope(_bdot(xb, wn_ref[:, 2 * ATT_W: 2 * ATT_W + LANES])).astype(kk_ref.dtype)

    c0 = 2 * ATT_W + LANES
    xh = xh_ref[...].astype(BF16)
    not_first = (i % tiles_per_seq != 0).astype(F32)
    for g in range(2 * ML_W // LANES // 2):
        cs = slice(c0 + g * 256, c0 + (g + 1) * 256)
        ls = slice(g * 256, (g + 1) * 256)
        pre = _bdot(xb, wn_ref[:, cs])
        ext_ref[0:8, :] = _bdot(xh, wn_ref[:, cs]) * not_first
        ext_ref[8:, :] = pre
        y = cb_ref[:, ls] + cw_ref[CONV_WIDTH - 1: CONV_WIDTH, ls] * pre
        for k in range(1, CONV_WIDTH):
            y = y + cw_ref[CONV_WIDTH - 1 - k: CONV_WIDTH - k, ls] * ext_ref[8 - k: 8 - k + TM, :]
        y = y * jax.nn.sigmoid(y)
        if g < 2:
            mq_ref[:, g * 256:(g + 1) * 256] = y.astype(mq_ref.dtype)
        else:
            mk_ref[:, (g - 2) * 256:(g - 1) * 256] = (y * (ML_HEAD_DIM ** -0.5)).astype(mk_ref.dtype)
    c1 = c0 + 2 * ML_W
    mv_ref[...] = _bdot(xb, wn_ref[:, c1: c1 + ML_W]).astype(mv_ref.dtype)
    og_ref[...] = jax.nn.sigmoid(_bdot(xb, wn_ref[:, c1 + ML_W: c1 + 2 * ML_W])).astype(og_ref.dtype)

    t = _dot_nt(wt_ref[...], xb)
    vt_ref[0] = t[:ATT_HEAD_DIM].astype(vt_ref.dtype)
    g = t[ATT_HEAD_DIM:] + gb_ref[...]
    row = lax.broadcasted_iota(I32, (16, 1), 0)
    gt_ref[0] = jnp.where(row < IDX_HEADS, g * IDX_W_SCALE,
                          jnp.where(row < IDX_HEADS + ML_HEADS, g,
                                    jnp.minimum(g, 0.0) - jnp.log1p(jnp.exp(-jnp.abs(g)))))


def _proj_call(x2, pos2, invf, wn, wt, cw, cb, gb, B, S):
    N = x2.shape[0]
    tps = S // TM
    ncol = wn.shape[1]
    row = lambda w: pl.BlockSpec((TM, w), lambda i: (i, 0))
    out_shape = [jax.ShapeDtypeStruct((N, ATT_W), BF16), jax.ShapeDtypeStruct((N, IDX_QW), BF16),
                 jax.ShapeDtypeStruct((N, LANES), BF16)] + [jax.ShapeDtypeStruct((N, ML_W), BF16)] * 4 + [
        jax.ShapeDtypeStruct((B, ATT_HEAD_DIM, S), BF16), jax.ShapeDtypeStruct((B, 16, S), F32)]
    out_specs = [row(ATT_W), row(IDX_QW), row(LANES)] + [row(ML_W)] * 4 + [
        pl.BlockSpec((1, ATT_HEAD_DIM, TM), lambda i: (i // tps, 0, i % tps)),
        pl.BlockSpec((1, 16, TM), lambda i: (i // tps, 0, i % tps))]
    return pl.pallas_call(
        functools.partial(_proj_kernel, tps),
        grid=(N // TM,),
        in_specs=[row(D_MODEL),
                  pl.BlockSpec((8, D_MODEL), lambda i: (jnp.maximum(i * (TM // 8) - 1, 0), 0)),
                  row(1), _const_spec((1, LANES)), _const_spec((D_MODEL, ncol)), _const_spec(wt.shape),
                  _const_spec(cw.shape), _const_spec(cb.shape), _const_spec(gb.shape)],
        out_specs=out_specs, out_shape=out_shape,
        scratch_shapes=[pltpu.VMEM((TM + 8, 256), F32)],
        compiler_params=pltpu.CompilerParams(dimension_semantics=("arbitrary",), vmem_limit_bytes=VMEM_LIMIT),
        name="proj",
    )(x2, x2, pos2, invf, wn, wt, cw, cb, gb)


def _stack_heads_t(q_ref, low_rows):
    qt = q_ref[0].astype(F32).T
    z = jnp.zeros((ATT_HEAD_DIM, TQ), F32)
    cols = []
    for h in range(ATT_HEADS):
        blk = qt[h * ATT_HEAD_DIM:(h + 1) * ATT_HEAD_DIM, :]
        cols.append(jnp.concatenate([blk, z] if low_rows else [z, blk], axis=0))
    return jnp.concatenate(cols, axis=1).astype(BF16)


def _attn_kernel(k_sel, idx_bits, qa_ref, qi_ref, kk_ref, vt_ref, gq_ref, o_ref,
                 sc_ref, s_ref):
    j = pl.program_id(1)
    nch = (j * TQ + TQ + KC - 1) // KC
    qpos = j * TQ + lax.broadcasted_iota(I32, (1, TQ), 1)
    kpos0 = lax.broadcasted_iota(I32, (KC, TQ), 0)
    qi_s = _stack_heads_t(qi_ref, False)
    wq = gq_ref[0, 0:IDX_HEADS, :]

    def fold(a, op):
        parts = [a[r * 8:(r + 1) * 8] for r in range(a.shape[0] // 8)]
        while len(parts) > 1:
            parts = [op(parts[i], parts[i + 1]) for i in range(0, len(parts), 2)]
        return parts[0]

    def key_dot(c, q_s):
        return _bdot(kk_ref[0, pl.ds(pl.multiple_of(c * KC, KC), KC), :], q_s)

    def phase_a(c, carry):
        smin, smax = carry
        lg = key_dot(c, qi_s)
        k0 = pl.multiple_of(c * KC, KC)
        sc = jnp.zeros((KC, TQ), F32)
        for h in range(IDX_HEADS):
            sc = sc + jnp.maximum(lg[:, h * TQ:(h + 1) * TQ], 0.0) * wq[h:h + 1, :]
        causal = kpos0 + k0 <= qpos
        sc_ref[pl.ds(k0, KC), :] = jnp.where(causal, sc, -jnp.inf)
        smin = jnp.minimum(smin, fold(jnp.where(causal, sc, jnp.inf), jnp.minimum))
        smax = jnp.maximum(smax, fold(jnp.where(causal, sc, -jnp.inf), jnp.maximum))
        return smin, smax

    smin, smax = lax.fori_loop(0, nch, phase_a, (jnp.full((8, TQ), jnp.inf, F32), jnp.full((8, TQ), -jnp.inf, F32)))

    def count(pred):
        def body(c, acc):
            k0 = pl.multiple_of(c * KC, KC)
            return acc + fold(pred(sc_ref[pl.ds(k0, KC), :], k0).astype(I32), jnp.add)
        acc = lax.fori_loop(0, nch, body, jnp.zeros((8, TQ), I32))
        return jnp.sum(acc, axis=0, keepdims=True)

    def max_below(bound):
        def body(c, acc):
            s = sc_ref[pl.ds(pl.multiple_of(c * KC, KC), KC), :]
            return jnp.maximum(acc, fold(jnp.where(s < bound, s, -jnp.inf), jnp.maximum))
        acc = lax.fori_loop(0, nch, body, jnp.full((8, TQ), -jnp.inf, F32))
        return jnp.max(acc, axis=0, keepdims=True)

    kq = jnp.minimum(qpos + 1, k_sel)

    def bisect(_, st):
        lo, hi, hi_ok, clo = st
        x = 0.5 * lo + 0.5 * hi
        c = count(lambda s, k0: s >= x)
        ge = c >= kq
        return jnp.where(ge, x, lo), jnp.where(ge, hi, x), jnp.where(ge, hi_ok, 1), jnp.where(ge, c, clo)

    st = (jnp.min(smin, axis=0, keepdims=True), jnp.max(smax, axis=0, keepdims=True),
          jnp.zeros((1, TQ), I32), qpos + 1)
    st = lax.fori_loop(0, N_BISECT, bisect, st)

    def exact_check(st, tie):
        lo, hi, hi_ok, clo = st
        rem = (clo != kq) & (tie == 0)
        v = max_below(jnp.where(hi_ok > 0, hi, jnp.inf))
        cv = count(lambda s, k0: s >= v)
        hit = rem & (cv >= kq)
        miss = rem & (cv < kq)
        st = (jnp.where(hit, v, lo), jnp.where(miss, v, hi), jnp.where(miss, 1, hi_ok), jnp.where(hit, kq, clo))
        return st, jnp.where(hit & (cv > kq), 1, tie)

    def flags(st, tie):
        return jnp.max(jnp.where((st[3] != kq) & (tie == 0), 2, 0) + tie)

    st, tie = exact_check(st, jnp.zeros((1, TQ), I32))

    def refine(carry):
        st, tie, _ = carry
        st, tie = exact_check(lax.fori_loop(0, 2, bisect, st), tie)
        return st, tie, flags(st, tie)

    st, tie, code = lax.while_loop(lambda c: c[2] >= 2, refine, (st, tie, flags(st, tie)))
    thr = st[0]

    def tie_break(_):
        need = kq - count(lambda s, k0: s > thr)
        def step(it, p):
            cand = p | jnp.left_shift(jnp.int32(1), idx_bits - 1 - it)
            cnt = count(lambda s, k0: (s == thr) & (kpos0 + k0 < cand))
            return jnp.where(cnt < need, cand, p)
        return lax.fori_loop(0, idx_bits, step, jnp.zeros((1, TQ), I32))

    plim = lax.cond(code > 0, tie_break, lambda _: jnp.full((1, TQ), INT_MAX, I32), 0)

    qa_s = _stack_heads_t(qa_ref, True)
    W = ATT_HEADS * TQ

    def logits(c, mx):
        s = key_dot(c, qa_s)
        k0 = pl.multiple_of(c * KC, KC)
        sc = sc_ref[pl.ds(k0, KC), :]
        sel = (sc > thr) | ((sc == thr) & (kpos0 + k0 <= plim))
        bias = jnp.where(sel, 0.0, -jnp.inf)
        cols = []
        for h in range(ATT_HEADS):
            sh = s[:, h * TQ:(h + 1) * TQ] + bias
            s_ref[pl.ds(k0, KC), h * TQ:(h + 1) * TQ] = sh
            cols.append(fold(sh, jnp.maximum))
        return jnp.maximum(mx, jnp.concatenate(cols, axis=1))

    mx = lax.fori_loop(0, nch, logits, jnp.full((8, W), -jnp.inf, F32))
    mx = jnp.max(mx, axis=0, keepdims=True)

    def weighted(c, carry):
        l, acc = carry
        k0 = pl.multiple_of(c * KC, KC)
        cols, ps = [], []
        for h in range(ATT_HEADS):
            hs = slice(h * TQ, (h + 1) * TQ)
            p = jnp.exp2(s_ref[pl.ds(k0, KC), hs] - mx[:, hs])
            cols.append(fold(p, jnp.add))
            ps.append(p.astype(BF16))
        acc = acc + _bdot(vt_ref[0, :, pl.ds(k0, KC)], jnp.concatenate(ps, axis=1))
        return l + jnp.concatenate(cols, axis=1), acc

    l, acc = lax.fori_loop(0, nch, weighted, (jnp.zeros((8, W), F32), jnp.zeros((ATT_HEAD_DIM, W), F32)))
    out_t = acc / jnp.sum(l, axis=0, keepdims=True)
    out_t = jnp.concatenate([out_t[:, h * TQ:(h + 1) * TQ] for h in range(ATT_HEADS)], axis=0)
    o_ref[0] = out_t.T.astype(o_ref.dtype)


def _attn_call(qa, qi, kk, vt, gt, B, S):
    k_sel = min(TOPK_MAX, S // 4)
    W = ATT_HEADS * TQ
    qspec =pl.BlockSpec((1, TQ, ATT_W), lambda b, j: (b, j, 0))
    return pl.pallas_call(
        functools.partial(_attn_kernel, k_sel, (S - 1).bit_length()),
        grid=(B, S // TQ),
        in_specs=[qspec, qspec,
                  pl.BlockSpec((1, S, LANES), lambda b, j: (b, 0, 0)),
                  pl.BlockSpec((1, ATT_HEAD_DIM, S), lambda b, j: (b, 0, 0)),
                  pl.BlockSpec((1, 16, TQ), lambda b, j: (b, 0, j))],
        out_specs=qspec,
        out_shape=jax.ShapeDtypeStruct((B, S, ATT_W), BF16),
        scratch_shapes=[pltpu.VMEM((S, TQ), F32), pltpu.VMEM((S, W), F32)],
        compiler_params=pltpu.CompilerParams(dimension_semantics=("arbitrary", "arbitrary"),
                                             vmem_limit_bytes=VMEM_LIMIT),
        name="attn",
    )(qa, qi, kk, vt, gt)


def _segment_scan(x, op, identity, seg):
    lane = lax.broadcasted_iota(I32, x.shape, 1) % seg
    s = 1
    while s < seg:
        x = op(x, jnp.where(lane >= s, pltpu.roll(x, s, 1), identity))
        s *= 2
    return x


def _split3(x):
    hi = x.astype(BF16)
    r = x - hi.astype(F32)
    mid = r.astype(BF16)
    return hi, mid, (r - mid.astype(F32)).astype(BF16)


def _mlstm_kernel(nchunks, mq_ref, mk_ref, mv_ref, og_ref, gt_ref, ng_ref, y_ref, c_ref, g_ref):
    L, H, Dh = ML_L, ML_HEADS, ML_HEAD_DIM
    c_ref[...] = jnp.zeros_like(c_ref)
    r_i = lax.broadcasted_iota(I32, (L, L), 0)
    c_i = lax.broadcasted_iota(I32, (L, L), 1)
    eye = (r_i == c_i).astype(BF16)
    eye3 = jnp.concatenate([eye, eye, eye], axis=1)
    tril = r_i >= c_i
    ones_b = jnp.ones((L, LANES), BF16)
    ones2 = jnp.ones((2 * Dh, LANES), BF16)
    heads = range(H)

    gi = gt_ref[0, IDX_HEADS:IDX_HEADS + H, :]
    lf = gt_ref[0, IDX_HEADS + H:IDX_HEADS + 2 * H, :]
    b_seq = _segment_scan(lf, jnp.add, 0.0, L)
    a_seq = gi - b_seq
    g_ref[...] = jnp.concatenate([b_seq, a_seq, _segment_scan(a_seq, jnp.maximum, -jnp.inf, L),
                                  jnp.zeros_like(b_seq)], axis=0)

    def lane_sum(x):
        hi = x.astype(BF16)
        return _bdot(jnp.concatenate([hi, (x - hi.astype(F32)).astype(BF16)], axis=1), ones2)

    def chunk(c, m_prev):
        t0 = pl.multiple_of(c * L, L)
        rows = pl.ds(t0, L)
        stacked = g_ref[:, rows]
        a_all = stacked[H:2 * H]
        m_last = jnp.maximum(stacked[2 * H:3 * H, L - 1:L], m_prev)
        m_next = stacked[0:H, L - 1:L] + m_last
        cols = _dot_nt(eye3, jnp.concatenate(_split3(stacked), axis=1))
        b_col = [cols[:, h:h + 1] for h in heads]
        a_col = [cols[:, H + h:H + h + 1] for h in heads]
        mcol = [jnp.maximum(cols[:, 2 * H + h:2 * H + h + 1], m_prev[h:h + 1]) for h in heads]
        hsl = [slice(h * Dh, (h + 1) * Dh) for h in heads]
        q = [mq_ref[0, rows, hsl[h]] for h in heads]
        k = [mk_ref[0, rows, hsl[h]] for h in heads]
        v_aug = [jnp.concatenate([mv_ref[0, rows, hsl[h]], ones_b], axis=1) for h in heads]
        cst = [c_ref[h] for h in heads]
        qk = [_dot_nt(q[h], k[h]) for h in heads]
        qc = [_bdot(q[h], cst[h].astype(BF16)) for h in heads]
        w_intra = [jnp.where(tril, jnp.exp(a_all[h:h + 1] - mcol[h]), 0.0) for h in heads]
        sv = [_bdot((qk[h] * w_intra[h]).astype(BF16), v_aug[h]) for h in heads]
        for h in heads:
            wv = (v_aug[h].astype(F32) * jnp.exp(a_col[h] - m_last[h:h + 1])).astype(BF16)
            c_ref[h] = jnp.exp(m_prev[h:h + 1] - m_last[h:h + 1]) * cst[h] + _dot_tn(k[h], wv)
        haug = [jnp.exp(m_prev[h:h + 1] - mcol[h]) * qc[h] + sv[h] for h in heads]
        hm = [haug[h][:, :Dh] / jnp.maximum(jnp.abs(haug[h][:, Dh:]), jnp.exp(-(b_col[h] + mcol[h]))) for h in heads]
        d = [hm[h] - lane_sum(hm[h]) * (1.0 / Dh) for h in heads]
        var = [lane_sum(d[h] * d[h]) * (1.0 / Dh) for h in heads]
        for h in heads:
            yn = d[h] * lax.rsqrt(var[h] + LN_EPS) * ng_ref[:, hsl[h]]
            y_ref[0, rows, hsl[h]] = (og_ref[0, rows, hsl[h]].astype(F32) * yn).astype(y_ref.dtype)
        return m_next

    lax.fori_loop(0, nchunks, chunk, jnp.zeros((H, 1), F32))


def _mlstm_call(mq, mk, mv, og, gt, ng, B, S):
    seq = pl.BlockSpec((1, S, ML_W), lambda b: (b, 0, 0))
    return pl.pallas_call(
        functools.partial(_mlstm_kernel, S // ML_L),
        grid=(B,),
        in_specs=[seq, seq, seq, seq, pl.BlockSpec((1, 16, S), lambda b: (b, 0, 0)), _const_spec(ng.shape)],
        out_specs=seq,
        out_shape=jax.ShapeDtypeStruct((B, S, ML_W), BF16),
        scratch_shapes=[pltpu.VMEM((ML_HEADS, ML_HEAD_DIM, 2 * ML_HEAD_DIM), F32),
                        pltpu.VMEM((4 * ML_HEADS, S), F32)],
        compiler_params=pltpu.CompilerParams(dimension_semantics=("arbitrary",), vmem_limit_bytes=VMEM_LIMIT),
        name="mlstm",
    )(mq, mk, mv, og, gt, ng)


def _merge_kernel(x_ref, ya_ref, yb_ref, wg_ref, wua_ref, wub_ref, wo_ref, g_ref, b_ref, h_ref):
    x = x_ref[...]
    xb = x.astype(BF16)
    ma = jax.nn.sigmoid(_bdot(xb, wg_ref[:, :D_MODEL])) * _bdot(ya_ref[...], wua_ref[...])
    mb = jax.nn.sigmoid(_bdot(xb, wg_ref[:, D_MODEL:])) * _bdot(yb_ref[...], wub_ref[...])
    r = DEEPNORM_ALPHA * x + _bdot((ma + mb).astype(BF16), wo_ref[...])
    h_ref[...] = _layer_norm(r, g_ref[...], b_ref[...])


def _merge_call(x2, ya, yb, wg, wua, wub, wo, g, b):
    N = x2.shape[0]
    row = lambda w: pl.BlockSpec((TM, w), lambda i: (i, 0))
    return pl.pallas_call(
        _merge_kernel,
        grid=(N // TM,),
        in_specs=[row(D_MODEL), row(ATT_W), row(ML_W)] + [_const_spec(a.shape) for a in (wg, wua, wub, wo, g, b)],
        out_specs=row(D_MODEL),
        out_shape=jax.ShapeDtypeStruct((N, D_MODEL), F32),
        compiler_params=pltpu.CompilerParams(dimension_semantics=("arbitrary",), vmem_limit_bytes=VMEM_LIMIT),
        name="merge",
    )(x2, ya, yb, wg, wua, wub, wo, g, b)


def _ffn_kernel(h_ref, p_ref, w1_ref, w2_ref, wpg_ref, wpp_ref, g_ref, b_ref, o_ref):
    h = h_ref[...]
    hb = h.astype(BF16)
    ff = jnp.zeros_like(h)
    for c in range(D_FF // D_MODEL):
        cs = slice(c * D_MODEL, (c + 1) * D_MODEL)
        a = jnp.maximum(_bdot(hb, w1_ref[:, cs]), 0.0)
        ff = ff + _bdot((a * a).astype(BF16), w2_ref[cs, :])
    r = DEEPNORM_ALPHA * h + ff
    r = r + jax.nn.sigmoid(_bdot(r.astype(BF16), wpg_ref[...])) * _bdot(p_ref[...].astype(BF16), wpp_ref[...])
    o_ref[...] = _layer_norm(r, g_ref[...], b_ref[...])


def _ffn_call(h1, p2, w1, w2, wpg, wpp, g, b):
    N = h1.shape[0]
    row = lambda w: pl.BlockSpec((TM, w), lambda i: (i, 0))
    return pl.pallas_call(
        _ffn_kernel,
        grid=(N // TM,),
        in_specs=[row(D_MODEL), row(PLE_DIM)] + [_const_spec(a.shape) for a in (w1, w2, wpg, wpp, g, b)],
        out_specs=row(D_MODEL),
        out_shape=jax.ShapeDtypeStruct((N, D_MODEL), F32),
        compiler_params=pltpu.CompilerParams(dimension_semantics=("arbitrary",), vmem_limit_bytes=VMEM_LIMIT),
        name="ffn",
    )(h1, p2, w1, w2, wpg, wpp, g, b)


def _split_w_in(w):
    parts, off = {}, 0
    for name, width in _SPLIT:
        parts[name] = w[:, off:off + width]
        off += width
    return parts


def kernel(x, p, positions, w_in, conv_w, conv_b, b_igate, b_fgate, ml_norm_g, w_up_a, w_up_b, w_out,
           ln1_g, ln1_b, w_ff1, w_ff2, w_ple_gate, w_ple_proj, ln2_g, ln2_b):
    B, S, D = x.shape
    assert D == D_MODEL and S % TM == 0 and S % KC == 0 and w_in.shape[0] == DEPTH == 1
    N = B * S
    x2 = x.reshape(N, D)
    pos2 = positions.reshape(N, 1)
    half = ATT_HEAD_DIM // 2
    inv_freq = 1.0 / (ROPE_THETA ** (jnp.arange(0, ATT_HEAD_DIM, 2, dtype=F32) / ATT_HEAD_DIM))
    invf = jnp.tile(inv_freq, LANES // half).reshape(1, LANES)

    w = _split_w_in(w_in[0])
    wn = jnp.concatenate([w['att_q'] * (ATT_HEAD_DIM ** -0.5 * LOG2E), w['idx_q'], w['att_k'], w['idx_k'],
                          w['ml_q'], w['ml_k'], w['ml_v'], w['ml_o']], axis=1).astype(BF16)
    wt = jnp.concatenate([w['att_v'], w['idx_w'], w['ml_i'], w['ml_f']], axis=1).T.astype(BF16)
    gb = jnp.concatenate([jnp.zeros((IDX_HEADS,), F32), b_igate[0], b_fgate[0]]).reshape(16, 1)
    wg = jnp.concatenate([w['gate_a'], w['gate_b']], axis=1).astype(BF16)

    qa, qi, kk, mq, mk, mv, og, vt, gt = _proj_call(x2, pos2, invf, wn, wt, conv_w[0], conv_b[0].reshape(1, -1),
                                                    gb, B, S)
    r3 = lambda a: a.reshape(B, S, a.shape[-1])
    ya = _attn_call(r3(qa), r3(qi), r3(kk), vt, gt, B, S)
    yb = _mlstm_call(r3(mq), r3(mk), r3(mv), r3(og), gt, ml_norm_g[0].reshape(1, -1), B, S)
    h1 = _merge_call(x2, ya.reshape(N, ATT_W), yb.reshape(N, ML_W), wg, w_up_a[0].astype(BF16),
                     w_up_b[0].astype(BF16), w_out[0].astype(BF16), ln1_g[0].reshape(1, -1), ln1_b[0].reshape(1, -1))
    out = _ffn_call(h1, p[0].reshape(N, PLE_DIM), w_ff1[0].astype(BF16), w_ff2[0].astype(BF16),
                    w_ple_gate[0].astype(BF16), w_ple_proj[0].astype(BF16),
                    ln2_g[0].reshape(1, -1), ln2_b[0].reshape(1, -1))
    return out.reshape(B, S, D)
```

```python
import functools

import jax
import jax.numpy as jnp
import numpy as np
from jax import lax
from jax.experimental import pallas as pl
from jax.experimental.pallas import tpu as pltpu

F32 = jnp.float32
BF16 = jnp.bfloat16
I32 = jnp.int32

D_MODEL = 1024
ATT_HEADS = 8
ATT_HEAD_DIM = 64
IDX_HEADS = 8
IDX_DIM = 64
TOPK_MAX = 256
ML_HEADS = 4
ML_HEAD_DIM = 128
CONV_WIDTH = 4
D_FF = 4 * D_MODEL
PLE_DIM = 256
ROPE_THETA = 10000.0
LN_EPS = 1e-5
DEPTH = 1
DEEPNORM_ALPHA = (2.0 * DEPTH) ** 0.25
IDX_W_SCALE = (IDX_HEADS ** -0.5) * (IDX_DIM ** -0.5)
LOG2E = 1.4426950408889634

ATT_W = ATT_HEADS * ATT_HEAD_DIM
IDX_QW = IDX_HEADS * IDX_DIM
ML_W = ML_HEADS * ML_HEAD_DIM
assert ATT_HEAD_DIM == IDX_DIM == 64 and ATT_HEADS == IDX_HEADS == 8

_SPLIT = (
    ('att_q', ATT_W), ('att_k', ATT_HEAD_DIM), ('att_v', ATT_HEAD_DIM),
    ('idx_q', IDX_QW), ('idx_k', IDX_DIM), ('idx_w', IDX_HEADS),
    ('ml_q', ML_W), ('ml_k', ML_W), ('ml_v', ML_W),
    ('ml_i', ML_HEADS), ('ml_f', ML_HEADS), ('ml_o', ML_W),
    ('gate_a', D_MODEL), ('gate_b', D_MODEL),
)

LANES = 128
VMEM_LIMIT = 56 * 1024 * 1024

TM = 512
TQ = 256
KC = 512
N_BISECT = 15
ML_L = 128


def _const_spec(shape):
    nd = len(shape)
    return pl.BlockSpec(shape, lambda *_: (0,) * nd, pipeline_mode=pl.Buffered(1))


def _bdot(a, b):
    return jnp.dot(a, b, preferred_element_type=F32)


def _dot_nt(a, b):
    return lax.dot_general(a, b, (((1,), (1,)), ((), ())), preferred_element_type=F32)


def _dot_tn(a, b):
    return lax.dot_general(a, b, (((0,), (0,)), ((), ())), preferred_element_type=F32)


def _layer_norm(r, g, b):
    mu = jnp.mean(r, axis=-1, keepdims=True)
    d = r - mu
    var = jnp.mean(d * d, axis=-1, keepdims=True)
    return d * lax.rsqrt(var + LN_EPS) * g + b


def _proj_kernel(tiles_per_seq, x_ref, xh_ref, pos_ref, invf_ref, wn_ref, wt_ref, cw_ref, cb_ref, gb_ref,
                 qa_ref, qi_ref, kk_ref, mq_ref, mk_ref, mv_ref, og_ref, vt_ref, gt_ref, ext_ref):
    i = pl.program_id(0)
    xb = x_ref[...].astype(BF16)
    ang = pos_ref[...].astype(F32) * invf_ref[...]
    lane = lax.broadcasted_iota(I32, (1, LANES), 1)
    first_half = (lane % ATT_HEAD_DIM) < (ATT_HEAD_DIM // 2)
    cos = jnp.cos(ang)
    sin = jnp.where(first_half, -jnp.sin(ang), jnp.sin(ang))

    def rope(v):
        swapped = jnp.where(first_half, pltpu.roll(v, LANES - 32, 1), pltpu.roll(v, 32, 1))
        return v * cos + swapped * sin

    def rope_wide(off, out_ref):
        for g in range(ATT_W // LANES):
            v = _bdot(xb, wn_ref[:, off + g * LANES: off + (g + 1) * LANES])
            out_ref[:, g * LANES:(g + 1) * LANES] = rope(v).astype(out_ref.dtype)

    rope_wide(0, qa_ref)
    rope_wide(ATT_W, qi_ref)
    kk_ref[...] = rope(_bdot(xb, wn_ref[:, 2 * ATT_W: 2 * ATT_W + LANES])).astype(kk_ref.dtype)

    c0 = 2 * ATT_W + LANES
    xh = xh_ref[...].astype(BF16)
    not_first = (i % tiles_per_seq != 0).astype(F32)
    for g in range(2 * ML_W // LANES // 2):
        cs = slice(c0 + g * 256, c0 + (g + 1) * 256)
        ls = slice(g * 256, (g + 1) * 256)
        pre = _bdot(xb, wn_ref[:, cs])
        ext_ref[0:8, :] = _bdot(xh, wn_ref[:, cs]) * not_first
        ext_ref[8:, :] = pre
        y = cb_ref[:, ls] + cw_ref[CONV_WIDTH - 1: CONV_WIDTH, ls] * pre
        for k in range(1, CONV_WIDTH):
            y = y + cw_ref[CONV_WIDTH - 1 - k: CONV_WIDTH - k, ls] * ext_ref[8 - k: 8 - k + TM, :]
        y = y * jax.nn.sigmoid(y)
        if g < 2:
            mq_ref[:, g * 256:(g + 1) * 256] = y.astype(mq_ref.dtype)
        else:
            mk_ref[:, (g - 2) * 256:(g - 1) * 256] = (y * (ML_HEAD_DIM ** -0.5)).astype(mk_ref.dtype)
    c1 = c0 + 2 * ML_W
    mv_ref[...] = _bdot(xb, wn_ref[:, c1: c1 + ML_W]).astype(mv_ref.dtype)
    og_ref[...] = jax.nn.sigmoid(_bdot(xb, wn_ref[:, c1 + ML_W: c1 + 2 * ML_W])).astype(og_ref.dtype)

    t = _dot_nt(wt_ref[...], xb)
    vt_ref[0] = t[:ATT_HEAD_DIM].astype(vt_ref.dtype)
    g = t[ATT_HEAD_DIM:] + gb_ref[...]
    row = lax.broadcasted_iota(I32, (16, 1), 0)
    gt_ref[0] = jnp.where(row < IDX_HEADS, g * IDX_W_SCALE,
                          jnp.where(row < IDX_HEADS + ML_HEADS, g,
                                    jnp.minimum(g, 0.0) - jnp.log1p(jnp.exp(-jnp.abs(g)))))


def _proj_call(x2, pos2, invf, wn, wt, cw, cb, gb, B, S):
    N = x2.shape[0]
    tps = S // TM
    ncol = wn.shape[1]
    row = lambda w: pl.BlockSpec((TM, w), lambda i: (i, 0))
    out_shape = [jax.ShapeDtypeStruct((N, ATT_W), BF16), jax.ShapeDtypeStruct((N, IDX_QW), BF16),
                 jax.ShapeDtypeStruct((N, LANES), BF16)] + [jax.ShapeDtypeStruct((N, ML_W), BF16)] * 4 + [
        jax.ShapeDtypeStruct((B, ATT_HEAD_DIM, S), BF16), jax.ShapeDtypeStruct((B, 16, S), F32)]
    out_specs = [row(ATT_W), row(IDX_QW), row(LANES)] + [row(ML_W)] * 4 + [
        pl.BlockSpec((1, ATT_HEAD_DIM, TM), lambda i: (i // tps, 0, i % tps)),
        pl.BlockSpec((1, 16, TM), lambda i: (i // tps, 0, i % tps))]
    return pl.pallas_call(
        functools.partial(_proj_kernel, tps),
        grid=(N // TM,),
        in_specs=[row(D_MODEL),
                  pl.BlockSpec((8, D_MODEL), lambda i: (jnp.maximum(i * (TM // 8) - 1, 0), 0)),
                  row(1), _const_spec((1, LANES)), _const_spec((D_MODEL, ncol)), _const_spec(wt.shape),
                  _const_spec(cw.shape), _const_spec(cb.shape), _const_spec(gb.shape)],
        out_specs=out_specs, out_shape=out_shape,
        scratch_shapes=[pltpu.VMEM((TM + 8, 256), F32)],
        compiler_params=pltpu.CompilerParams(dimension_semantics=("arbitrary",), vmem_limit_bytes=VMEM_LIMIT),
        name="proj",
    )(x2, x2, pos2, invf, wn, wt, cw, cb, gb)


def _stack_heads_t(q_ref, low_rows):
    qt = q_ref[0].astype(F32).T
    z = jnp.zeros((ATT_HEAD_DIM, TQ), F32)
    cols = []
    for h in range(ATT_HEADS):
        blk = qt[h * ATT_HEAD_DIM:(h + 1) * ATT_HEAD_DIM, :]
        cols.append(jnp.concatenate([blk, z] if low_rows else [z, blk], axis=0))
    return jnp.concatenate(cols, axis=1).astype(BF16)


def _attn_kernel(k_sel, qa_ref, qi_ref, kk_ref, vt_ref, gq_ref, o_ref,
                 sc_ref, s_ref):
    j = pl.program_id(1)
    nch = (j * TQ + TQ + KC - 1) // KC
    qpos = j * TQ + lax.broadcasted_iota(I32, (1, TQ), 1)
    kpos0 = lax.broadcasted_iota(I32, (KC, TQ), 0)
    qi_s = _stack_heads_t(qi_ref, False)
    wq = gq_ref[0, 0:IDX_HEADS, :]

    def fold(a, op):
        out = None
        for r0 in range(0, a.shape[0], 64):
            parts = [a[r0 + r * 8: r0 + (r + 1) * 8] for r in range(8)]
            while len(parts) > 1:
                parts = [op(parts[i], parts[i + 1]) for i in range(0, len(parts), 2)]
            out = parts[0] if out is None else op(out, parts[0])
        return out

    def key_dot(c, q_s):
        return _bdot(kk_ref[0, pl.ds(pl.multiple_of(c * KC, KC), KC), :], q_s)

    def phase_a(c, carry):
        smin, smax = carry
        lg = key_dot(c, qi_s)
        k0 = pl.multiple_of(c * KC, KC)
        sc = jnp.zeros((KC, TQ), F32)
        for h in range(IDX_HEADS):
            sc = sc + jnp.maximum(lg[:, h * TQ:(h + 1) * TQ], 0.0) * wq[h:h + 1, :]
        causal = kpos0 + k0 <= qpos
        sc_ref[pl.ds(k0, KC), :] = jnp.where(causal, sc, -jnp.inf)
        smin = jnp.minimum(smin, fold(jnp.where(causal, sc, jnp.inf), jnp.minimum))
        smax = jnp.maximum(smax, fold(jnp.where(causal, sc, -jnp.inf), jnp.maximum))
        return smin, smax

    smin, smax = lax.fori_loop(0, nch, phase_a, (jnp.full((8, TQ), jnp.inf, F32), jnp.full((8, TQ), -jnp.inf, F32)))

    def count(pred):
        def body(c, acc):
            k0 = pl.multiple_of(c * KC, KC)
            return acc + fold(pred(sc_ref[pl.ds(k0, KC), :], k0).astype(I32), jnp.add)
        acc = lax.fori_loop(0, nch, body, jnp.zeros((8, TQ), I32))
        return jnp.sum(acc, axis=0, keepdims=True)

    def max_below(bound):
        def body(c, acc):
            s = sc_ref[pl.ds(pl.multiple_of(c * KC, KC), KC), :]
            return jnp.maximum(acc, fold(jnp.where(s < bound, s, -jnp.inf), jnp.maximum))
        acc = lax.fori_loop(0, nch, body, jnp.full((8, TQ), -jnp.inf, F32))
        return jnp.max(acc, axis=0, keepdims=True)

    kq = jnp.minimum(qpos + 1, k_sel)

    def bisect(_, st):
        lo, hi, hi_ok, clo = st
        x = 0.5 * lo + 0.5 * hi
        c = count(lambda s, k0: s >= x)
        ge = c >= kq
        return jnp.where(ge, x, lo), jnp.where(ge, hi, x), jnp.where(ge, hi_ok, 1), jnp.where(ge, c, clo)

    st = (jnp.min(smin, axis=0, keepdims=True), jnp.max(smax, axis=0, keepdims=True),
          jnp.zeros((1, TQ), I32), qpos + 1)
    st = lax.fori_loop(0, N_BISECT, bisect, st)

    def exact_check(st, tie):
        lo, hi, hi_ok, clo = st
        rem = (clo != kq) & (tie == 0)
        v = max_below(jnp.where(hi_ok > 0, hi, jnp.inf))
        cv = count(lambda s, k0: s >= v)
        hit = rem & (cv >= kq)
        miss = rem & (cv < kq)
        st = (jnp.where(hit, v, lo), jnp.where(miss, v, hi), jnp.where(miss, 1, hi_ok), jnp.where(hit, kq, clo))
        return st, jnp.where(hit & (cv > kq), 1, tie)

    def flags(st, tie):
        return jnp.max(jnp.where((st[3] != kq) & (tie == 0), 2, 0) + tie)

    st, tie = exact_check(st, jnp.zeros((1, TQ), I32))

    def refine(carry):
        st, tie, _ = carry
        st, tie = exact_check(lax.fori_loop(0, 2, bisect, st), tie)
        return st, tie, flags(st, tie)

    st, tie, code = lax.while_loop(lambda c: c[2] >= 2, refine, (st, tie, flags(st, tie)))
    thr = st[0]

    def tie_fix(_):
        need = (kq - count(lambda s, k0: s > thr)).astype(F32)
        tri = (lax.broadcasted_iota(I32, (KC, KC), 0) >= lax.broadcasted_iota(I32, (KC, KC), 1)).astype(BF16)

        def body(c, base):
            rows = pl.ds(pl.multiple_of(c * KC, KC), KC)
            s = sc_ref[rows, :]
            eq = (s == thr) & (tie > 0)
            rank = _bdot(tri, eq.astype(BF16)) + base
            sc_ref[rows, :] = jnp.where(eq & (rank > need), -jnp.inf, s)
            return rank[KC - 1:KC, :]

        lax.fori_loop(0, nch, body, jnp.zeros((1, TQ), F32))
        return 0

    lax.cond(code > 0, tie_fix, lambda _: 0, 0)

    qa_s = _stack_heads_t(qa_ref, True)
    W = ATT_HEADS * TQ

    def logits(c, mx):
        s = key_dot(c, qa_s)
        k0 = pl.multiple_of(c * KC, KC)
        bias = jnp.where(sc_ref[pl.ds(k0, KC), :] >= thr, 0.0, -jnp.inf)
        cols = []
        for h in range(ATT_HEADS):
            sh = s[:, h * TQ:(h + 1) * TQ] + bias
            s_ref[pl.ds(k0, KC), h * TQ:(h + 1) * TQ] = sh
            cols.append(fold(sh, jnp.maximum))
        return jnp.maximum(mx, jnp.concatenate(cols, axis=1))

    mx = lax.fori_loop(0, nch, logits, jnp.full((8, W), -jnp.inf, F32))
    mx = jnp.max(mx, axis=0, keepdims=True)

    def weighted(c, carry):
        l, acc = carry
        k0 = pl.multiple_of(c * KC, KC)
        cols, ps = [], []
        for h in range(ATT_HEADS):
            hs = slice(h * TQ, (h + 1) * TQ)
            p = jnp.exp2(s_ref[pl.ds(k0, KC), hs] - mx[:, hs])
            cols.append(fold(p, jnp.add))
            ps.append(p.astype(BF16))
        acc = acc + _bdot(vt_ref[0, :, pl.ds(k0, KC)], jnp.concatenate(ps, axis=1))
        return l + jnp.concatenate(cols, axis=1), acc

    l, acc = lax.fori_loop(0, nch, weighted, (jnp.zeros((8, W), F32), jnp.zeros((ATT_HEAD_DIM, W), F32)))
    out_t = acc / jnp.sum(l, axis=0, keepdims=True)
    out_t = jnp.concatenate([out_t[:, h * TQ:(h + 1) * TQ] for h in range(ATT_HEADS)], axis=0)
    o_ref[0] = out_t.T.astype(o_ref.dtype)


def _attn_call(qa, qi, kk, vt, gt, B, S):
    k_sel = min(TOPK_MAX, S // 4)
    W = ATT_HEADS * TQ
    qspec =pl.BlockSpec((1, TQ, ATT_W), lambda b, j: (b, j, 0))
    return pl.pallas_call(
        functools.partial(_attn_kernel, k_sel),
        grid=(B, S // TQ),
        in_specs=[qspec, qspec,
                  pl.BlockSpec((1, S, LANES), lambda b, j: (b, 0, 0)),
                  pl.BlockSpec((1, ATT_HEAD_DIM, S), lambda b, j: (b, 0, 0)),
                  pl.BlockSpec((1, 16, TQ), lambda b, j: (b, 0, j))],
        out_specs=qspec,
        out_shape=jax.ShapeDtypeStruct((B, S, ATT_W), BF16),
        scratch_shapes=[pltpu.VMEM((S, TQ), F32), pltpu.VMEM((S, W), F32)],
        compiler_params=pltpu.CompilerParams(dimension_semantics=("arbitrary", "arbitrary"),
                                             vmem_limit_bytes=VMEM_LIMIT),
        name="attn",
    )(qa, qi, kk, vt, gt)


def _segment_scan(x, op, identity, seg):
    lane = lax.broadcasted_iota(I32, x.shape, 1) % seg
    s = 1
    while s < seg:
        x = op(x, jnp.where(lane >= s, pltpu.roll(x, s, 1), identity))
        s *= 2
    return x


def _split3(x):
    hi = x.astype(BF16)
    r = x - hi.astype(F32)
    mid = r.astype(BF16)
    return hi, mid, (r - mid.astype(F32)).astype(BF16)


def _mlstm_kernel(nchunks, mq_ref, mk_ref, mv_ref, og_ref, gt_ref, ng_ref, y_ref, c_ref, g_ref):
    L, H, Dh = ML_L, ML_HEADS, ML_HEAD_DIM
    c_ref[...] = jnp.zeros_like(c_ref)
    r_i = lax.broadcasted_iota(I32, (L, L), 0)
    c_i = lax.broadcasted_iota(I32, (L, L), 1)
    eye = (r_i == c_i).astype(BF16)
    eye3 = jnp.concatenate([eye, eye, eye], axis=1)
    tril = r_i >= c_i
    ones_b = jnp.ones((L, LANES), BF16)
    ones2 = jnp.ones((2 * Dh, LANES), BF16)
    heads = range(H)

    gi = gt_ref[0, IDX_HEADS:IDX_HEADS + H, :]
    lf = gt_ref[0, IDX_HEADS + H:IDX_HEADS + 2 * H, :]
    b_seq = _segment_scan(lf, jnp.add, 0.0, L)
    a_seq = gi - b_seq
    g_ref[...] = jnp.concatenate([b_seq, a_seq, _segment_scan(a_seq, jnp.maximum, -jnp.inf, L),
                                  jnp.zeros_like(b_seq)], axis=0)

    def lane_sum(x):
        hi = x.astype(BF16)
        return _bdot(jnp.concatenate([hi, (x - hi.astype(F32)).astype(BF16)], axis=1), ones2)

    def chunk(c, m_prev):
        t0 = pl.multiple_of(c * L, L)
        rows = pl.ds(t0, L)
        stacked = g_ref[:, rows]
        a_all = stacked[H:2 * H]
        m_last = jnp.maximum(stacked[2 * H:3 * H, L - 1:L], m_prev)
        m_next = stacked[0:H, L - 1:L] + m_last
        cols = _dot_nt(eye3, jnp.concatenate(_split3(stacked), axis=1))
        b_col = [cols[:, h:h + 1] for h in heads]
        a_col = [cols[:, H + h:H + h + 1] for h in heads]
        mcol = [jnp.maximum(cols[:, 2 * H + h:2 * H + h + 1], m_prev[h:h + 1]) for h in heads]
        hsl = [slice(h * Dh, (h + 1) * Dh) for h in heads]
        q = [mq_ref[0, rows, hsl[h]] for h in heads]
        k = [mk_ref[0, rows, hsl[h]] for h in heads]
        v_aug = [jnp.concatenate([mv_ref[0, rows, hsl[h]], ones_b], axis=1) for h in heads]
        cst = [c_ref[h] for h in heads]
        qk = [_dot_nt(q[h], k[h]) for h in heads]
        qc = [_bdot(q[h], cst[h].astype(BF16)) for h in heads]
        w_intra = [jnp.where(tril, jnp.exp(a_all[h:h + 1] - mcol[h]), 0.0) for h in heads]
        sv = [_bdot((qk[h] * w_intra[h]).astype(BF16), v_aug[h]) for h in heads]
        for h in heads:
            wv = (v_aug[h].astype(F32) * jnp.exp(a_col[h] - m_last[h:h + 1])).astype(BF16)
            c_ref[h] = jnp.exp(m_prev[h:h + 1] - m_last[h:h + 1]) * cst[h] + _dot_tn(k[h], wv)
        haug = [jnp.exp(m_prev[h:h + 1] - mcol[h]) * qc[h] + sv[h] for h in heads]
        hm = [haug[h][:, :Dh] / jnp.maximum(jnp.abs(haug[h][:, Dh:]), jnp.exp(-(b_col[h] + mcol[h]))) for h in heads]
        d = [hm[h] - lane_sum(hm[h]) * (1.0 / Dh) for h in heads]
        var = [lane_sum(d[h] * d[h]) * (1.0 / Dh) for h in heads]
        for h in heads:
            yn = d[h] * lax.rsqrt(var[h] + LN_EPS) * ng_ref[:, hsl[h]]
            y_ref[0, rows, hsl[h]] = (og_ref[0, rows, hsl[h]].astype(F32) * yn).astype(y_ref.dtype)
        return m_next

    lax.fori_loop(0, nchunks, chunk, jnp.zeros((H, 1), F32))


def _mlstm_call(mq, mk, mv, og, gt, ng, B, S):
    seq = pl.BlockSpec((1, S, ML_W), lambda b: (b, 0, 0))
    return pl.pallas_call(
        functools.partial(_mlstm_kernel, S // ML_L),
        grid=(B,),
        in_specs=[seq, seq, seq, seq, pl.BlockSpec((1, 16, S), lambda b: (b, 0, 0)), _const_spec(ng.shape)],
        out_specs=seq,
        out_shape=jax.ShapeDtypeStruct((B, S, ML_W), BF16),
        scratch_shapes=[pltpu.VMEM((ML_HEADS, ML_HEAD_DIM, 2 * ML_HEAD_DIM), F32),
                        pltpu.VMEM((4 * ML_HEADS, S), F32)],
        compiler_params=pltpu.CompilerParams(dimension_semantics=("arbitrary",), vmem_limit_bytes=VMEM_LIMIT),
        name="mlstm",
    )(mq, mk, mv, og, gt, ng)


def _merge_kernel(x_ref, ya_ref, yb_ref, wg_ref, wua_ref, wub_ref, wo_ref, g_ref, b_ref, h_ref):
    x = x_ref[...]
    xb = x.astype(BF16)
    ma = jax.nn.sigmoid(_bdot(xb, wg_ref[:, :D_MODEL])) * _bdot(ya_ref[...], wua_ref[...])
    mb = jax.nn.sigmoid(_bdot(xb, wg_ref[:, D_MODEL:])) * _bdot(yb_ref[...], wub_ref[...])
    r = DEEPNORM_ALPHA * x + _bdot((ma + mb).astype(BF16), wo_ref[...])
    h_ref[...] = _layer_norm(r, g_ref[...], b_ref[...])


def _merge_call(x2, ya, yb, wg, wua, wub, wo, g, b):
    N = x2.shape[0]
    row = lambda w: pl.BlockSpec((TM, w), lambda i: (i, 0))
    return pl.pallas_call(
        _merge_kernel,
        grid=(N // TM,),
        in_specs=[row(D_MODEL), row(ATT_W), row(ML_W)] + [_const_spec(a.shape) for a in (wg, wua, wub, wo, g, b)],
        out_specs=row(D_MODEL),
        out_shape=jax.ShapeDtypeStruct((N, D_MODEL), F32),
        compiler_params=pltpu.CompilerParams(dimension_semantics=("arbitrary",), vmem_limit_bytes=VMEM_LIMIT),
        name="merge",
    )(x2, ya, yb, wg, wua, wub, wo, g, b)


def _ffn_kernel(h_ref, p_ref, w1_ref, w2_ref, wpg_ref, wpp_ref, g_ref, b_ref, o_ref):
    h = h_ref[...]
    hb = h.astype(BF16)
    ff = jnp.zeros_like(h)
    for c in range(D_FF // D_MODEL):
        cs = slice(c * D_MODEL, (c + 1) * D_MODEL)
        a = jnp.maximum(_bdot(hb, w1_ref[:, cs]), 0.0)
        ff = ff + _bdot((a * a).astype(BF16), w2_ref[cs, :])
    r = DEEPNORM_ALPHA * h + ff
    r = r + jax.nn.sigmoid(_bdot(r.astype(BF16), wpg_ref[...])) * _bdot(p_ref[...].astype(BF16), wpp_ref[...])
    o_ref[...] = _layer_norm(r, g_ref[...], b_ref[...])


def _ffn_call(h1, p2, w1, w2, wpg, wpp, g, b):
    N = h1.shape[0]
    row = lambda w: pl.BlockSpec((TM, w), lambda i: (i, 0))
    return pl.pallas_call(
        _ffn_kernel,
        grid=(N // TM,),
        in_specs=[row(D_MODEL), row(PLE_DIM)] + [_const_spec(a.shape) for a in (w1, w2, wpg, wpp, g, b)],
        out_specs=row(D_MODEL),
        out_shape=jax.ShapeDtypeStruct((N, D_MODEL), F32),
        compiler_params=pltpu.CompilerParams(dimension_semantics=("arbitrary",), vmem_limit_bytes=VMEM_LIMIT),
        name="ffn",
    )(h1, p2, w1, w2, wpg, wpp, g, b)


def _split_w_in(w):
    parts, off = {}, 0
    for name, width in _SPLIT:
        parts[name] = w[:, off:off + width]
        off += width
    return parts


def kernel(x, p, positions, w_in, conv_w, conv_b, b_igate, b_fgate, ml_norm_g, w_up_a, w_up_b, w_out,
           ln1_g, ln1_b, w_ff1, w_ff2, w_ple_gate, w_ple_proj, ln2_g, ln2_b):
    B, S, D = x.shape
    assert D == D_MODEL and S % TM == 0 and S % KC == 0 and w_in.shape[0] == DEPTH == 1
    N = B * S
    x2 = x.reshape(N, D)
    pos2 = positions.reshape(N, 1)
    half = ATT_HEAD_DIM // 2
    inv_freq = 1.0 / (ROPE_THETA ** (jnp.arange(0, ATT_HEAD_DIM, 2, dtype=F32) / ATT_HEAD_DIM))
    invf = jnp.tile(inv_freq, LANES // half).reshape(1, LANES)

    w = _split_w_in(w_in[0])
    wn = jnp.concatenate([w['att_q'] * (ATT_HEAD_DIM ** -0.5 * LOG2E), w['idx_q'], w['att_k'], w['idx_k'],
                          w['ml_q'], w['ml_k'], w['ml_v'], w['ml_o']], axis=1).astype(BF16)
    wt = jnp.concatenate([w['att_v'], w['idx_w'], w['ml_i'], w['ml_f']], axis=1).T.astype(BF16)
    gb = jnp.concatenate([jnp.zeros((IDX_HEADS,), F32), b_igate[0], b_fgate[0]]).reshape(16, 1)
    wg = jnp.concatenate([w['gate_a'], w['gate_b']], axis=1).astype(BF16)

    qa, qi, kk, mq, mk, mv, og, vt, gt = _proj_call(x2, pos2, invf, wn, wt, conv_w[0], conv_b[0].reshape(1, -1),
                                                    gb, B, S)
    r3 = lambda a: a.reshape(B, S, a.shape[-1])
    ya = _attn_call(r3(qa), r3(qi), r3(kk), vt, gt, B, S)
    yb = _mlstm_call(r3(mq), r3(mk), r3(mv), r3(og), gt, ml_norm_g[0].reshape(1, -1), B, S)
    h1 = _merge_call(x2, ya.reshape(N, ATT_W), yb.reshape(N, ML_W), wg, w_up_a[0].astype(BF16),
                     w_up_b[0].astype(BF16), w_out[0].astype(BF16), ln1_g[0].reshape(1, -1), ln1_b[0].reshape(1, -1))
    out = _ffn_call(h1, p[0].reshape(N, PLE_DIM), w_ff1[0].astype(BF16), w_ff2[0].astype(BF16),
                    w_ple_gate[0].astype(BF16), w_ple_proj[0].astype(BF16),
                    ln2_g[0].reshape(1, -1), ln2_b[0].reshape(1, -1))
    return out.reshape(B, S, D)
```

```python
import functools

import jax
import jax.numpy as jnp
import numpy as np
from jax import lax
from jax.experimental import pallas as pl
from jax.experimental.pallas import tpu as pltpu

F32 = jnp.float32
BF16 = jnp.bfloat16
I32 = jnp.int32

D_MODEL = 1024
ATT_HEADS = 8
ATT_HEAD_DIM = 64
IDX_HEADS = 8
IDX_DIM = 64
TOPK_MAX = 256
ML_HEADS = 4
ML_HEAD_DIM = 128
CONV_WIDTH = 4
D_FF = 4 * D_MODEL
PLE_DIM = 256
ROPE_THETA = 10000.0
LN_EPS = 1e-5
DEPTH = 1
DEEPNORM_ALPHA = (2.0 * DEPTH) ** 0.25
IDX_W_SCALE = (IDX_HEADS ** -0.5) * (IDX_DIM ** -0.5)
LOG2E = 1.4426950408889634

ATT_W = ATT_HEADS * ATT_HEAD_DIM
IDX_QW = IDX_HEADS * IDX_DIM
ML_W = ML_HEADS * ML_HEAD_DIM
assert ATT_HEAD_DIM == IDX_DIM == 64 and ATT_HEADS == IDX_HEADS == 8

_SPLIT = (
    ('att_q', ATT_W), ('att_k', ATT_HEAD_DIM), ('att_v', ATT_HEAD_DIM),
    ('idx_q', IDX_QW), ('idx_k', IDX_DIM), ('idx_w', IDX_HEADS),
    ('ml_q', ML_W), ('ml_k', ML_W), ('ml_v', ML_W),
    ('ml_i', ML_HEADS), ('ml_f', ML_HEADS), ('ml_o', ML_W),
    ('gate_a', D_MODEL), ('gate_b', D_MODEL),
)

LANES = 128
VMEM_LIMIT = 56 * 1024 * 1024

TM = 512
TQ = 256
KC = 512
N_BISECT = 15
ML_L = 128


def _const_spec(shape):
    nd = len(shape)
    return pl.BlockSpec(shape, lambda *_: (0,) * nd, pipeline_mode=pl.Buffered(1))


def _bdot(a, b):
    return jnp.dot(a, b, preferred_element_type=F32)


def _dot_nt(a, b):
    return lax.dot_general(a, b, (((1,), (1,)), ((), ())), preferred_element_type=F32)


def _dot_tn(a, b):
    return lax.dot_general(a, b, (((0,), (0,)), ((), ())), preferred_element_type=F32)


def _layer_norm(r, g, b):
    mu = jnp.mean(r, axis=-1, keepdims=True)
    d = r - mu
    var = jnp.mean(d * d, axis=-1, keepdims=True)
    return d * lax.rsqrt(var + LN_EPS) * g + b


def _proj_kernel(tiles_per_seq, x_ref, pos_ref, invf_ref, wt_ref, wn_ref, cw_ref, cb_ref, gb_ref,
                 qs_ref, kka_ref, kki_ref, mq_ref, mk_ref, mv_ref, og_ref, vt_ref, gt_ref, ext_ref, halo_ref):
    i = pl.program_id(0)
    xb = x_ref[...].astype(BF16)
    HD, HALF = ATT_HEAD_DIM, ATT_HEAD_DIM // 2
    t = _dot_nt(wt_ref[...], xb)
    ang = invf_ref[...] * pos_ref[0].astype(F32)
    cos, sin = jnp.cos(ang), jnp.sin(ang)

    def rope(blk):
        x1, x2 = t[blk * HD: blk * HD + HALF], t[blk * HD + HALF: (blk + 1) * HD]
        return x1 * cos - x2 * sin, x2 * cos + x1 * sin

    for h in range(ATT_HEADS):
        pieces = rope(h) + rope(ATT_HEADS + h)
        for g in range(TM // TQ):
            for r, piece in enumerate(pieces):
                qs_ref[g, r * HALF:(r + 1) * HALF, h * TQ:(h + 1) * TQ] = piece[:, g * TQ:(g + 1) * TQ].astype(BF16)
    kt = jnp.concatenate(rope(2 * ATT_HEADS) + rope(2 * ATT_HEADS + 1), axis=0).T
    is_att = lax.broadcasted_iota(I32, (1, LANES), 1) < HD
    kka_ref[...] = jnp.where(is_att, kt, 0.0).astype(BF16)
    kki_ref[...] = jnp.where(is_att, 0.0, kt).astype(BF16)

    c0 = 0
    not_first = i % tiles_per_seq != 0

    @pl.when(i == 0)
    def _():
        halo_ref[...] = jnp.zeros(halo_ref.shape, F32)

    for g in range(2 * ML_W // LANES // 2):
        cs = slice(c0 + g * 256, c0 + (g + 1) * 256)
        ls = slice(g * 256, (g + 1) * 256)
        pre = _bdot(xb, wn_ref[:, cs])
        ext_ref[0:8, :] = jnp.where(not_first, halo_ref[g], 0.0)
        ext_ref[8:, :] = pre
        halo_ref[g] = pre[TM - 8:, :]
        y = cb_ref[:, ls] + cw_ref[CONV_WIDTH - 1: CONV_WIDTH, ls] * pre
        for k in range(1, CONV_WIDTH):
            y = y + cw_ref[CONV_WIDTH - 1 - k: CONV_WIDTH - k, ls] * ext_ref[8 - k: 8 - k + TM, :]
        y = y * jax.nn.sigmoid(y)
        if g < 2:
            mq_ref[:, g * 256:(g + 1) * 256] = y.astype(mq_ref.dtype)
        else:
            mk_ref[:, (g - 2) * 256:(g - 1) * 256] = (y * (ML_HEAD_DIM ** -0.5)).astype(mk_ref.dtype)
    c1 = c0 + 2 * ML_W
    mv_ref[...] = _bdot(xb, wn_ref[:, c1: c1 + ML_W]).astype(mv_ref.dtype)
    og_ref[...] = jax.nn.sigmoid(_bdot(xb, wn_ref[:, c1 + ML_W: c1 + 2 * ML_W])).astype(og_ref.dtype)

    r0 = (2 * ATT_HEADS + 2) * HD
    vt_ref[0] = t[r0: r0 + HD].astype(vt_ref.dtype)
    g = t[r0 + HD:] + gb_ref[...]
    row = lax.broadcasted_iota(I32, (16, 1), 0)
    gt_ref[0] = jnp.where(row < IDX_HEADS, g * IDX_W_SCALE,
                          jnp.where(row < IDX_HEADS + ML_HEADS, g,
                                    jnp.minimum(g, 0.0) - jnp.log1p(jnp.exp(-jnp.abs(g)))))


def _proj_call(x2, pos3, invf, wt, wn, cw, cb, gb, B, S):
    N = x2.shape[0]
    tps = S // TM
    W = ATT_HEADS * TQ
    row = lambda w: pl.BlockSpec((TM, w), lambda i: (i, 0))
    out_shape = [jax.ShapeDtypeStruct((N // TQ, LANES, W), BF16), jax.ShapeDtypeStruct((N, LANES), BF16),
                 jax.ShapeDtypeStruct((N, LANES), BF16)] + [jax.ShapeDtypeStruct((N, ML_W), BF16)] * 4 + [
        jax.ShapeDtypeStruct((B, ATT_HEAD_DIM, S), BF16), jax.ShapeDtypeStruct((B, 16, S), F32)]
    out_specs = [pl.BlockSpec((TM // TQ, LANES, W), lambda i: (i, 0, 0)), row(LANES), row(LANES)] + [row(ML_W)] * 4 + [
        pl.BlockSpec((1, ATT_HEAD_DIM, TM), lambda i: (i // tps, 0, i % tps)),
        pl.BlockSpec((1, 16, TM), lambda i: (i // tps, 0, i % tps))]
    return pl.pallas_call(
        functools.partial(_proj_kernel, tps),
        grid=(N // TM,),
        in_specs=[row(D_MODEL), pl.BlockSpec((1, 1, TM), lambda i: (i, 0, 0)), _const_spec(invf.shape),
                  _const_spec(wt.shape), _const_spec(wn.shape),
                  _const_spec(cw.shape), _const_spec(cb.shape), _const_spec(gb.shape)],
        out_specs=out_specs, out_shape=out_shape,
        scratch_shapes=[pltpu.VMEM((TM + 8, 256), F32), pltpu.VMEM((2 * ML_W // 256, 8, 256), F32)],
        compiler_params=pltpu.CompilerParams(dimension_semantics=("arbitrary",), vmem_limit_bytes=VMEM_LIMIT),
        name="proj",
    )(x2, pos3, invf, wt, wn, cw, cb, gb)


def _attn_kernel(k_sel, qs_ref, kka_ref, kki_ref, vt_ref, gq_ref, o_ref, sc_ref, s_ref):
    j = pl.program_id(1)
    nch = (j * TQ + TQ + KC - 1) // KC
    qpos = j * TQ + lax.broadcasted_iota(I32, (1, TQ), 1)
    kpos0 = lax.broadcasted_iota(I32, (KC, TQ), 0)
    wq = gq_ref[0, 0:IDX_HEADS, :]

    def fold(a, op):
        out = None
        for r0 in range(0, a.shape[0], 64):
            parts = [a[r0 + r * 8: r0 + (r + 1) * 8] for r in range(8)]
            while len(parts) > 1:
                parts = [op(parts[i], parts[i + 1]) for i in range(0, len(parts), 2)]
            out = parts[0] if out is None else op(out, parts[0])
        return out

    def key_dot(c, k_ref):
        return _bdot(k_ref[0, pl.ds(pl.multiple_of(c * KC, KC), KC), :], qs_ref[0])

    def phase_a(c, carry):
        smin, smax = carry
        lg = key_dot(c, kki_ref)
        k0 = pl.multiple_of(c * KC, KC)
        sc = jnp.zeros((KC, TQ), F32)
        for h in range(IDX_HEADS):
            sc = sc + jnp.maximum(lg[:, h * TQ:(h + 1) * TQ], 0.0) * wq[h:h + 1, :]
        causal = kpos0 + k0 <= qpos
        sc_ref[pl.ds(k0, KC), :] = jnp.where(causal, sc, -jnp.inf)
        smin = jnp.minimum(smin, fold(jnp.where(causal, sc, jnp.inf), jnp.minimum))
        smax = jnp.maximum(smax, fold(jnp.where(causal, sc, -jnp.inf), jnp.maximum))
        return smin, smax

    smin, smax = lax.fori_loop(0, nch, phase_a, (jnp.full((8, TQ), jnp.inf, F32), jnp.full((8, TQ), -jnp.inf, F32)))

    SUB = 64

    def count(pred):
        def body(c, acc):
            k0 = pl.multiple_of(c * KC, KC)
            for r0 in range(0, KC, SUB):
                acc = acc + fold(pred(sc_ref[pl.ds(k0 + r0, SUB), :], k0 + r0).astype(I32), jnp.add)
            return acc
        acc = lax.fori_loop(0, nch, body, jnp.zeros((8, TQ), I32))
        return jnp.sum(acc, axis=0, keepdims=True)

    def max_below(bound):
        def body(c, acc):
            k0 = pl.multiple_of(c * KC, KC)
            for r0 in range(0, KC, SUB):
                s = sc_ref[pl.ds(k0 + r0, SUB), :]
                acc = jnp.maximum(acc, fold(jnp.where(s < bound, s, -jnp.inf), jnp.maximum))
            return acc
        acc = lax.fori_loop(0, nch, body, jnp.full((8, TQ), -jnp.inf, F32))
        return jnp.max(acc, axis=0, keepdims=True)

    kq = jnp.minimum(qpos + 1, k_sel)

    def bisect(_, st):
        lo, hi, hi_ok, clo = st
        x = 0.5 * lo + 0.5 * hi
        c = count(lambda s, k0: s >= x)
        ge = c >= kq
        return jnp.where(ge, x, lo), jnp.where(ge, hi, x), jnp.where(ge, hi_ok, 1), jnp.where(ge, c, clo)

    st = (jnp.min(smin, axis=0, keepdims=True), jnp.max(smax, axis=0, keepdims=True),
          jnp.zeros((1, TQ), I32), qpos + 1)
    st = lax.fori_loop(0, N_BISECT, bisect, st)

    def exact_check(st, tie):
        lo, hi, hi_ok, clo = st
        rem = (clo != kq) & (tie == 0)
        v = max_below(jnp.where(hi_ok > 0, hi, jnp.inf))
        cv = count(lambda s, k0: s >= v)
        hit = rem & (cv >= kq)
        miss = rem & (cv < kq)
        st = (jnp.where(hit, v, lo), jnp.where(miss, v, hi), jnp.where(miss, 1, hi_ok), jnp.where(hit, kq, clo))
        return st, jnp.where(hit & (cv > kq), 1, tie)

    def flags(st, tie):
        return jnp.max(jnp.where((st[3] != kq) & (tie == 0), 2, 0) + tie)

    st, tie = exact_check(st, jnp.zeros((1, TQ), I32))

    def refine(carry):
        st, tie, _ = carry
        st, tie = exact_check(lax.fori_loop(0, 2, bisect, st), tie)
        return st, tie, flags(st, tie)

    st, tie, code = lax.while_loop(lambda c: c[2] >= 2, refine, (st, tie, flags(st, tie)))
    thr = st[0]

    def tie_fix(_):
        need = (kq - count(lambda s, k0: s > thr)).astype(F32)
        tri = (lax.broadcasted_iota(I32, (KC, KC), 0) >= lax.broadcasted_iota(I32, (KC, KC), 1)).astype(BF16)

        def body(c, base):
            rows = pl.ds(pl.multiple_of(c * KC, KC), KC)
            s = sc_ref[rows, :]
            eq = (s == thr) & (tie > 0)
            rank = _bdot(tri, eq.astype(BF16)) + base
            sc_ref[rows, :] = jnp.where(eq & (rank > need), -jnp.inf, s)
            return rank[KC - 1:KC, :]

        lax.fori_loop(0, nch, body, jnp.zeros((1, TQ), F32))
        return 0

    lax.cond(code > 0, tie_fix, lambda _: 0, 0)

    W = ATT_HEADS * TQ

    def logits(c, mx):
        s = key_dot(c, kka_ref)
        k0 = pl.multiple_of(c * KC, KC)
        bias = jnp.where(sc_ref[pl.ds(k0, KC), :] >= thr, 0.0, -jnp.inf)
        cols = []
        for h in range(ATT_HEADS):
            sh = s[:, h * TQ:(h + 1) * TQ] + bias
            s_ref[pl.ds(k0, KC), h * TQ:(h + 1) * TQ] = sh
            cols.append(fold(sh, jnp.maximum))
        return jnp.maximum(mx, jnp.concatenate(cols, axis=1))

    mx = lax.fori_loop(0, nch, logits, jnp.full((8, W), -jnp.inf, F32))
    mx = jnp.max(mx, axis=0, keepdims=True)

    def weighted(c, carry):
        l, acc = carry
        k0 = pl.multiple_of(c * KC, KC)
        cols, ps = [], []
        for h in range(ATT_HEADS):
            hs = slice(h * TQ, (h + 1) * TQ)
            p = jnp.exp2(s_ref[pl.ds(k0, KC), hs] - mx[:, hs])
            cols.append(fold(p, jnp.add))
            ps.append(p.astype(BF16))
        acc = acc + _bdot(vt_ref[0, :, pl.ds(k0, KC)], jnp.concatenate(ps, axis=1))
        return l + jnp.concatenate(cols, axis=1), acc

    l, acc = lax.fori_loop(0, nch, weighted, (jnp.zeros((8, W), F32), jnp.zeros((ATT_HEAD_DIM, W), F32)))
    out_t = acc / jnp.sum(l, axis=0, keepdims=True)
    out_t = jnp.concatenate([out_t[:, h * TQ:(h + 1) * TQ] for h in range(ATT_HEADS)], axis=0)
    o_ref[0] = out_t.T.astype(o_ref.dtype)


def _attn_call(qs, kka, kki, vt, gt, B, S):
    k_sel = min(TOPK_MAX, S // 4)
    W = ATT_HEADS * TQ
    nblk = S // TQ
    kspec = pl.BlockSpec((1, S, LANES), lambda b, j: (b, 0, 0))
    return pl.pallas_call(
        functools.partial(_attn_kernel, k_sel),
        grid=(B, nblk),
        in_specs=[pl.BlockSpec((1, LANES, W), lambda b, j: (b * nblk + j, 0, 0)), kspec, kspec,
                  pl.BlockSpec((1, ATT_HEAD_DIM, S), lambda b, j: (b, 0, 0)),
                  pl.BlockSpec((1, 16, TQ), lambda b, j: (b, 0, j))],
        out_specs=pl.BlockSpec((1, TQ, ATT_W), lambda b, j: (b, j, 0)),
        out_shape=jax.ShapeDtypeStruct((B, S, ATT_W), BF16),
        scratch_shapes=[pltpu.VMEM((S, TQ), F32), pltpu.VMEM((S, W), F32)],
        compiler_params=pltpu.CompilerParams(dimension_semantics=("arbitrary", "arbitrary"),
                                             vmem_limit_bytes=VMEM_LIMIT),
        name="attn",
    )(qs, kka, kki, vt, gt)


def _segment_scan(x, op, identity, seg):
    lane = lax.broadcasted_iota(I32, x.shape, 1) % seg
    s = 1
    while s < seg:
        x = op(x, jnp.where(lane >= s, pltpu.roll(x, s, 1), identity))
        s *= 2
    return x


def _split3(x):
    hi = x.astype(BF16)
    r = x - hi.astype(F32)
    mid = r.astype(BF16)
    return hi, mid, (r - mid.astype(F32)).astype(BF16)


def _mlstm_kernel(nchunks, mq_ref, mk_ref, mv_ref, og_ref, gt_ref, ng_ref, y_ref, c_ref, g_ref):
    L, H, Dh = ML_L, ML_HEADS, ML_HEAD_DIM
    c_ref[...] = jnp.zeros_like(c_ref)
    r_i = lax.broadcasted_iota(I32, (L, L), 0)
    c_i = lax.broadcasted_iota(I32, (L, L), 1)
    eye = (r_i == c_i).astype(BF16)
    eye3 = jnp.concatenate([eye, eye, eye], axis=1)
    tril = r_i >= c_i
    ones_b = jnp.ones((L, LANES), BF16)
    ones2 = jnp.ones((2 * Dh, LANES), BF16)
    heads = range(H)

    gi = gt_ref[0, IDX_HEADS:IDX_HEADS + H, :]
    lf = gt_ref[0, IDX_HEADS + H:IDX_HEADS + 2 * H, :]
    b_seq = _segment_scan(lf, jnp.add, 0.0, L)
    a_seq = gi - b_seq
    g_ref[...] = jnp.concatenate([b_seq, a_seq, _segment_scan(a_seq, jnp.maximum, -jnp.inf, L),
                                  jnp.zeros_like(b_seq)], axis=0)

    def lane_sum(x):
        hi = x.astype(BF16)
        return _bdot(jnp.concatenate([hi, (x - hi.astype(F32)).astype(BF16)], axis=1), ones2)

    def chunk(c, m_prev):
        t0 = pl.multiple_of(c * L, L)
        rows = pl.ds(t0, L)
        stacked = g_ref[:, rows]
        a_all = stacked[H:2 * H]
        m_last = jnp.maximum(stacked[2 * H:3 * H, L - 1:L], m_prev)
        m_next = stacked[0:H, L - 1:L] + m_last
        cols = _dot_nt(eye3, jnp.concatenate(_split3(stacked), axis=1))
        b_col = [cols[:, h:h + 1] for h in heads]
        a_col = [cols[:, H + h:H + h + 1] for h in heads]
        mcol = [jnp.maximum(cols[:, 2 * H + h:2 * H + h + 1], m_prev[h:h + 1]) for h in heads]
        hsl = [slice(h * Dh, (h + 1) * Dh) for h in heads]
        q = [mq_ref[0, rows, hsl[h]] for h in heads]
        k = [mk_ref[0, rows, hsl[h]] for h in heads]
        v_aug = [jnp.concatenate([mv_ref[0, rows, hsl[h]], ones_b], axis=1) for h in heads]
        cst = [c_ref[h] for h in heads]
        qk = [_dot_nt(q[h], k[h]) for h in heads]
        qc = [_bdot(q[h], cst[h].astype(BF16)) for h in heads]
        w_intra = [jnp.where(tril, jnp.exp(a_all[h:h + 1] - mcol[h]), 0.0) for h in heads]
        sv = [_bdot((qk[h] * w_intra[h]).astype(BF16), v_aug[h]) for h in heads]
        for h in heads:
            wv = (v_aug[h].astype(F32) * jnp.exp(a_col[h] - m_last[h:h + 1])).astype(BF16)
            c_ref[h] = jnp.exp(m_prev[h:h + 1] - m_last[h:h + 1]) * cst[h] + _dot_tn(k[h], wv)
        haug = [jnp.exp(m_prev[h:h + 1] - mcol[h]) * qc[h] + sv[h] for h in heads]
        hm = [haug[h][:, :Dh] / jnp.maximum(jnp.abs(haug[h][:, Dh:]), jnp.exp(-(b_col[h] + mcol[h]))) for h in heads]
        d = [hm[h] - lane_sum(hm[h]) * (1.0 / Dh) for h in heads]
        var = [lane_sum(d[h] * d[h]) * (1.0 / Dh) for h in heads]
        for h in heads:
            yn = d[h] * lax.rsqrt(var[h] + LN_EPS) * ng_ref[:, hsl[h]]
            y_ref[0, rows, hsl[h]] = (og_ref[0, rows, hsl[h]].astype(F32) * yn).astype(y_ref.dtype)
        return m_next

    lax.fori_loop(0, nchunks, chunk, jnp.zeros((H, 1), F32))


def _mlstm_call(mq, mk, mv, og, gt, ng, B, S):
    seq = pl.BlockSpec((1, S, ML_W), lambda b: (b, 0, 0))
    return pl.pallas_call(
        functools.partial(_mlstm_kernel, S // ML_L),
        grid=(B,),
        in_specs=[seq, seq, seq, seq, pl.BlockSpec((1, 16, S), lambda b: (b, 0, 0)), _const_spec(ng.shape)],
        out_specs=seq,
        out_shape=jax.ShapeDtypeStruct((B, S, ML_W), BF16),
        scratch_shapes=[pltpu.VMEM((ML_HEADS, ML_HEAD_DIM, 2 * ML_HEAD_DIM), F32),
                        pltpu.VMEM((4 * ML_HEADS, S), F32)],
        compiler_params=pltpu.CompilerParams(dimension_semantics=("arbitrary",), vmem_limit_bytes=VMEM_LIMIT),
        name="mlstm",
    )(mq, mk, mv, og, gt, ng)


def _merge_kernel(x_ref, ya_ref, yb_ref, wg_ref, wua_ref, wub_ref, wo_ref, g_ref, b_ref, h_ref):
    x = x_ref[...]
    xb = x.astype(BF16)
    ma = jax.nn.sigmoid(_bdot(xb, wg_ref[:, :D_MODEL])) * _bdot(ya_ref[...], wua_ref[...])
    mb = jax.nn.sigmoid(_bdot(xb, wg_ref[:, D_MODEL:])) * _bdot(yb_ref[...], wub_ref[...])
    r = DEEPNORM_ALPHA * x + _bdot((ma + mb).astype(BF16), wo_ref[...])
    h_ref[...] = _layer_norm(r, g_ref[...], b_ref[...])


def _merge_call(x2, ya, yb, wg, wua, wub, wo, g, b):
    N = x2.shape[0]
    row = lambda w: pl.BlockSpec((TM, w), lambda i: (i, 0))
    return pl.pallas_call(
        _merge_kernel,
        grid=(N // TM,),
        in_specs=[row(D_MODEL), row(ATT_W), row(ML_W)] + [_const_spec(a.shape) for a in (wg, wua, wub, wo, g, b)],
        out_specs=row(D_MODEL),
        out_shape=jax.ShapeDtypeStruct((N, D_MODEL), F32),
        compiler_params=pltpu.CompilerParams(dimension_semantics=("arbitrary",), vmem_limit_bytes=VMEM_LIMIT),
        name="merge",
    )(x2, ya, yb, wg, wua, wub, wo, g, b)


def _ffn_kernel(h_ref, p_ref, w1_ref, w2_ref, wpg_ref, wpp_ref, g_ref, b_ref, o_ref):
    h = h_ref[...]
    hb = h.astype(BF16)
    ff = jnp.zeros_like(h)
    for c in range(D_FF // D_MODEL):
        cs = slice(c * D_MODEL, (c + 1) * D_MODEL)
        a = jnp.maximum(_bdot(hb, w1_ref[:, cs]), 0.0)
        ff = ff + _bdot((a * a).astype(BF16), w2_ref[cs, :])
    r = DEEPNORM_ALPHA * h + ff
    r = r + jax.nn.sigmoid(_bdot(r.astype(BF16), wpg_ref[...])) * _bdot(p_ref[...].astype(BF16), wpp_ref[...])
    o_ref[...] = _layer_norm(r, g_ref[...], b_ref[...])


def _ffn_call(h1, p2, w1, w2, wpg, wpp, g, b):
    N = h1.shape[0]
    row = lambda w: pl.BlockSpec((TM, w), lambda i: (i, 0))
    return pl.pallas_call(
        _ffn_kernel,
        grid=(N // TM,),
        in_specs=[row(D_MODEL), row(PLE_DIM)] + [_const_spec(a.shape) for a in (w1, w2, wpg, wpp, g, b)],
        out_specs=row(D_MODEL),
        out_shape=jax.ShapeDtypeStruct((N, D_MODEL), F32),
        compiler_params=pltpu.CompilerParams(dimension_semantics=("arbitrary",), vmem_limit_bytes=VMEM_LIMIT),
        name="ffn",
    )(h1, p2, w1, w2, wpg, wpp, g, b)


def _split_w_in(w):
    parts, off = {}, 0
    for name, width in _SPLIT:
        parts[name] = w[:, off:off + width]
        off += width
    return parts


def kernel(x, p, positions, w_in, conv_w, conv_b, b_igate, b_fgate, ml_norm_g, w_up_a, w_up_b, w_out,
           ln1_g, ln1_b, w_ff1, w_ff2, w_ple_gate, w_ple_proj, ln2_g, ln2_b):
    B, S, D = x.shape
    assert D == D_MODEL and S % TM == 0 and S % KC == 0 and w_in.shape[0] == DEPTH == 1
    N = B * S
    x2 = x.reshape(N, D)
    pos3 = positions.reshape(N // TM, 1, TM)
    inv_freq = 1.0 / (ROPE_THETA ** (jnp.arange(0, ATT_HEAD_DIM, 2, dtype=F32) / ATT_HEAD_DIM))
    invf = inv_freq.reshape(ATT_HEAD_DIM // 2, 1)

    w = _split_w_in(w_in[0])
    wt = jnp.concatenate([w['att_q'] * (ATT_HEAD_DIM ** -0.5 * LOG2E), w['idx_q'], w['att_k'], w['idx_k'],
                          w['att_v'], w['idx_w'], w['ml_i'], w['ml_f']], axis=1).T.astype(BF16)
    wn = jnp.concatenate([w['ml_q'], w['ml_k'], w['ml_v'], w['ml_o']], axis=1).astype(BF16)
    gb = jnp.concatenate([jnp.zeros((IDX_HEADS,), F32), b_igate[0], b_fgate[0]]).reshape(16, 1)
    wg = jnp.concatenate([w['gate_a'], w['gate_b']], axis=1).astype(BF16)

    qs, kka, kki, mq, mk, mv, og, vt, gt = _proj_call(x2, pos3, invf, wt, wn, conv_w[0], conv_b[0].reshape(1, -1),
                                                      gb, B, S)
    r3 = lambda a: a.reshape(B, S, a.shape[-1])
    ya = _attn_call(qs, r3(kka), r3(kki), vt, gt, B, S)
    yb = _mlstm_call(r3(mq), r3(mk), r3(mv), r3(og), gt, ml_norm_g[0].reshape(1, -1), B, S)
    h1 = _merge_call(x2, ya.reshape(N, ATT_W), yb.reshape(N, ML_W), wg, w_up_a[0].astype(BF16),
                     w_up_b[0].astype(BF16), w_out[0].astype(BF16), ln1_g[0].reshape(1, -1), ln1_b[0].reshape(1, -1))
    out = _ffn_call(h1, p[0].reshape(N, PLE_DIM), w_ff1[0].astype(BF16), w_ff2[0].astype(BF16),
                    w_ple_gate[0].astype(BF16), w_ple_proj[0].astype(BF16),
                    ln2_g[0].reshape(1, -1), ln2_b[0].reshape(1, -1))
    return out.reshape(B, S, D)
```

```python
import functools

import jax
import jax.numpy as jnp
import numpy as np
from jax import lax
from jax.experimental import pallas as pl
from jax.experimental.pallas import tpu as pltpu

F32 = jnp.float32
BF16 = jnp.bfloat16
I32 = jnp.int32

D_MODEL = 1024
ATT_HEADS = 8
ATT_HEAD_DIM = 64
IDX_HEADS = 8
IDX_DIM = 64
TOPK_MAX = 256
ML_HEADS = 4
ML_HEAD_DIM = 128
CONV_WIDTH = 4
D_FF = 4 * D_MODEL
PLE_DIM = 256
ROPE_THETA = 10000.0
LN_EPS = 1e-5
DEPTH = 1
DEEPNORM_ALPHA = (2.0 * DEPTH) ** 0.25
IDX_W_SCALE = (IDX_HEADS ** -0.5) * (IDX_DIM ** -0.5)
LOG2E = 1.4426950408889634

ATT_W = ATT_HEADS * ATT_HEAD_DIM
IDX_QW = IDX_HEADS * IDX_DIM
ML_W = ML_HEADS * ML_HEAD_DIM
assert ATT_HEAD_DIM == IDX_DIM == 64 and ATT_HEADS == IDX_HEADS == 8

_SPLIT = (
    ('att_q', ATT_W), ('att_k', ATT_HEAD_DIM), ('att_v', ATT_HEAD_DIM),
    ('idx_q', IDX_QW), ('idx_k', IDX_DIM), ('idx_w', IDX_HEADS),
    ('ml_q', ML_W), ('ml_k', ML_W), ('ml_v', ML_W),
    ('ml_i', ML_HEADS), ('ml_f', ML_HEADS), ('ml_o', ML_W),
    ('gate_a', D_MODEL), ('gate_b', D_MODEL),
)

LANES = 128
VMEM_LIMIT = 56 * 1024 * 1024

TM = 512
TM_FFN = 1024
TQ = 256
KC = 512
MAX_REFINE = 160
assert KC == 2 * TQ
N_BISECT = 15
ML_L = 128


def _const_spec(shape):
    nd = len(shape)
    return pl.BlockSpec(shape, lambda *_: (0,) * nd, pipeline_mode=pl.Buffered(1))


def _bdot(a, b):
    return jnp.dot(a, b, preferred_element_type=F32)


def _dot_nt(a, b):
    return lax.dot_general(a, b, (((1,), (1,)), ((), ())), preferred_element_type=F32)


def _dot_tn(a, b):
    return lax.dot_general(a, b, (((0,), (0,)), ((), ())), preferred_element_type=F32)


def _layer_norm(r, g, b):
    mu = jnp.mean(r, axis=-1, keepdims=True)
    d = r - mu
    var = jnp.mean(d * d, axis=-1, keepdims=True)
    return d * lax.rsqrt(var + LN_EPS) * g + b


def _proj_kernel(tiles_per_seq, x_ref, pos_ref, invf_ref, wt_ref, wn_ref, cw_ref, cb_ref, gb_ref,
                 qs_ref, kka_ref, kki_ref, mq_ref, mk_ref, mv_ref, og_ref, vt_ref, gt_ref, ext_ref, halo_ref):
    i = pl.program_id(0)
    xb = x_ref[...].astype(BF16)
    HD, HALF = ATT_HEAD_DIM, ATT_HEAD_DIM // 2
    t = _dot_nt(wt_ref[...], xb)
    ang = invf_ref[...] * pos_ref[0].astype(F32)
    cos, sin = jnp.cos(ang), jnp.sin(ang)

    def rope(blk):
        x1, x2 = t[blk * HD: blk * HD + HALF], t[blk * HD + HALF: (blk + 1) * HD]
        return x1 * cos - x2 * sin, x2 * cos + x1 * sin

    for h in range(ATT_HEADS):
        pieces = rope(h) + rope(ATT_HEADS + h)
        for g in range(TM // TQ):
            for r, piece in enumerate(pieces):
                qs_ref[g, r * HALF:(r + 1) * HALF, h * TQ:(h + 1) * TQ] = piece[:, g * TQ:(g + 1) * TQ].astype(BF16)
    kt = jnp.concatenate(rope(2 * ATT_HEADS) + rope(2 * ATT_HEADS + 1), axis=0).T
    is_att = lax.broadcasted_iota(I32, (1, LANES), 1) < HD
    kka_ref[...] = jnp.where(is_att, kt, 0.0).astype(BF16)
    kki_ref[...] = jnp.where(is_att, 0.0, kt).astype(BF16)

    c0 = 0
    not_first = i % tiles_per_seq != 0

    @pl.when(i == 0)
    def _():
        halo_ref[...] = jnp.zeros(halo_ref.shape, F32)

    for g in range(2 * ML_W // LANES // 2):
        cs = slice(c0 + g * 256, c0 + (g + 1) * 256)
        ls = slice(g * 256, (g + 1) * 256)
        pre = _bdot(xb, wn_ref[:, cs])
        ext_ref[0:8, :] = jnp.where(not_first, halo_ref[g], 0.0)
        ext_ref[8:, :] = pre
        halo_ref[g] = pre[TM - 8:, :]
        y = cb_ref[:, ls] + cw_ref[CONV_WIDTH - 1: CONV_WIDTH, ls] * pre
        for k in range(1, CONV_WIDTH):
            y = y + cw_ref[CONV_WIDTH - 1 - k: CONV_WIDTH - k, ls] * ext_ref[8 - k: 8 - k + TM, :]
        y = y * jax.nn.sigmoid(y)
        if g < 2:
            mq_ref[:, g * 256:(g + 1) * 256] = y.astype(mq_ref.dtype)
        else:
            mk_ref[:, (g - 2) * 256:(g - 1) * 256] = (y * (ML_HEAD_DIM ** -0.5)).astype(mk_ref.dtype)
    c1 = c0 + 2 * ML_W
    mv_ref[...] = _bdot(xb, wn_ref[:, c1: c1 + ML_W]).astype(mv_ref.dtype)
    og_ref[...] = jax.nn.sigmoid(_bdot(xb, wn_ref[:, c1 + ML_W: c1 + 2 * ML_W])).astype(og_ref.dtype)

    r0 = (2 * ATT_HEADS + 2) * HD
    vt_ref[0] = t[r0: r0 + HD].astype(vt_ref.dtype)
    g = t[r0 + HD:] + gb_ref[...]
    row = lax.broadcasted_iota(I32, (16, 1), 0)
    gt_ref[0] = jnp.where(row < IDX_HEADS, g * IDX_W_SCALE,
                          jnp.where(row < IDX_HEADS + ML_HEADS, g,
                                    jnp.minimum(g, 0.0) - jnp.log1p(jnp.exp(-jnp.abs(g)))))


def _proj_call(x2, pos3, invf, wt, wn, cw, cb, gb, B, S):
    N = x2.shape[0]
    tps = S // TM
    W = ATT_HEADS * TQ
    row = lambda w: pl.BlockSpec((TM, w), lambda i: (i, 0))
    out_shape = [jax.ShapeDtypeStruct((N // TQ, LANES, W), BF16), jax.ShapeDtypeStruct((N, LANES), BF16),
                 jax.ShapeDtypeStruct((N, LANES), BF16)] + [jax.ShapeDtypeStruct((N, ML_W), BF16)] * 4 + [
        jax.ShapeDtypeStruct((B, ATT_HEAD_DIM, S), BF16), jax.ShapeDtypeStruct((B, 16, S), F32)]
    out_specs = [pl.BlockSpec((TM // TQ, LANES, W), lambda i: (i, 0, 0)), row(LANES), row(LANES)] + [row(ML_W)] * 4 + [
        pl.BlockSpec((1, ATT_HEAD_DIM, TM), lambda i: (i // tps, 0, i % tps)),
        pl.BlockSpec((1, 16, TM), lambda i: (i // tps, 0, i % tps))]
    return pl.pallas_call(
        functools.partial(_proj_kernel, tps),
        grid=(N // TM,),
        in_specs=[row(D_MODEL), pl.BlockSpec((1, 1, TM), lambda i: (i, 0, 0)), _const_spec(invf.shape),
                  _const_spec(wt.shape), _const_spec(wn.shape),
                  _const_spec(cw.shape), _const_spec(cb.shape), _const_spec(gb.shape)],
        out_specs=out_specs, out_shape=out_shape,
        scratch_shapes=[pltpu.VMEM((TM + 8, 256), F32), pltpu.VMEM((2 * ML_W // 256, 8, 256), F32)],
        compiler_params=pltpu.CompilerParams(dimension_semantics=("arbitrary",), vmem_limit_bytes=VMEM_LIMIT),
        name="proj",
    )(x2, pos3, invf, wt, wn, cw, cb, gb)


def _attn_kernel(k_sel, qs_ref, kka_ref, kki_ref, vt_ref, gq_ref, o_ref, sc_ref, s_ref):
    j = pl.program_id(1)
    nkeys = (j + 1) * TQ
    nfull = nkeys // KC
    qpos = j * TQ + lax.broadcasted_iota(I32, (1, TQ), 1)
    wq = gq_ref[0, 0:IDX_HEADS, :]
    W = ATT_HEADS * TQ

    def for_chunks(body, carry):
        carry = lax.fori_loop(0, nfull, lambda c, cr: body(pl.multiple_of(c * KC, KC), KC, cr), carry)
        return lax.cond(nkeys % KC != 0, lambda cr: body(pl.multiple_of(nfull * KC, KC), TQ, cr), lambda cr: cr, carry)

    def fold(a, op):
        out = None
        for r0 in range(0, a.shape[0], 64):
            parts = [a[r0 + r * 8: r0 + (r + 1) * 8] for r in range(8)]
            while len(parts) > 1:
                parts = [op(parts[i], parts[i + 1]) for i in range(0, len(parts), 2)]
            out = parts[0] if out is None else op(out, parts[0])
        return out

    def phase_a(k0, size, carry):
        smin, smax = carry
        kc = kki_ref[0, pl.ds(k0, size), :]
        sc = jnp.zeros((size, TQ), F32)
        for h in range(IDX_HEADS):
            sc = sc + jnp.maximum(_bdot(kc, qs_ref[0, :, h * TQ:(h + 1) * TQ]), 0.0) * wq[h:h + 1, :]
        causal = lax.broadcasted_iota(I32, (size, TQ), 0) + k0 <= qpos
        sc_ref[pl.ds(k0, size), :] = jnp.where(causal, sc, -jnp.inf)
        smin = jnp.minimum(smin, fold(jnp.where(causal, sc, jnp.inf), jnp.minimum))
        smax = jnp.maximum(smax, fold(jnp.where(causal, sc, -jnp.inf), jnp.maximum))
        return smin, smax

    smin, smax = for_chunks(phase_a, (jnp.full((8, TQ), jnp.inf, F32), jnp.full((8, TQ), -jnp.inf, F32)))

    SUB = 64

    def count(pred):
        def body(k0, size, acc):
            for r0 in range(0, size, SUB):
                acc = acc + fold(pred(sc_ref[pl.ds(k0 + r0, SUB), :]).astype(I32), jnp.add)
            return acc
        return jnp.sum(for_chunks(body, jnp.zeros((8, TQ), I32)), axis=0, keepdims=True)

    def max_below(bound):
        def body(k0, size, acc):
            for r0 in range(0, size, SUB):
                s = sc_ref[pl.ds(k0 + r0, SUB), :]
                acc = jnp.maximum(acc, fold(jnp.where(s < bound, s, -jnp.inf), jnp.maximum))
            return acc
        return jnp.max(for_chunks(body, jnp.full((8, TQ), -jnp.inf, F32)), axis=0, keepdims=True)

    kq = jnp.minimum(qpos + 1, k_sel)

    def bisect(_, st):
        lo, hi, hi_ok, clo = st
        x = 0.5 * lo + 0.5 * hi
        c = count(lambda s: s >= x)
        ge = c >= kq
        return jnp.where(ge, x, lo), jnp.where(ge, hi, x), jnp.where(ge, hi_ok, 1), jnp.where(ge, c, clo)

    st = (jnp.min(smin, axis=0, keepdims=True), jnp.max(smax, axis=0, keepdims=True),
          jnp.zeros((1, TQ), I32), qpos + 1)
    st = lax.fori_loop(0, N_BISECT, bisect, st)

    def exact_check(st, tie):
        lo, hi, hi_ok, clo = st
        rem = (clo != kq) & (tie == 0)
        v = max_below(jnp.where(hi_ok > 0, hi, jnp.inf))
        cv = count(lambda s: s >= v)
        hit = rem & (cv >= kq)
        miss = rem & (cv < kq)
        st = (jnp.where(hit, v, lo), jnp.where(miss, v, hi), jnp.where(miss, 1, hi_ok), jnp.where(hit, kq, clo))
        return st, jnp.where(hit & (cv > kq), 1, tie)

    def flags(st, tie):
        return jnp.max(jnp.where((st[3] != kq) & (tie == 0), 2, 0) + tie)

    st, tie = exact_check(st, jnp.zeros((1, TQ), I32))

    def refine(carry):
        st, tie, _, trips = carry
        st, tie = exact_check(lax.fori_loop(0, 2, bisect, st), tie)
        return st, tie, flags(st, tie), trips + 1

    st, tie, code, _ = lax.while_loop(lambda c: (c[2] >= 2) & (c[3] < MAX_REFINE), refine,
                                      (st, tie, flags(st, tie), 0))
    thr = st[0]

    def tie_fix(_):
        need = (kq - count(lambda s: s > thr)).astype(F32)

        def body(k0, size, base):
            tri = lax.broadcasted_iota(I32, (size, size), 0) >= lax.broadcasted_iota(I32, (size, size), 1)
            s = sc_ref[pl.ds(k0, size), :]
            eq = (s == thr) & (tie > 0)
            rank = _bdot(tri.astype(BF16), eq.astype(BF16)) + base
            sc_ref[pl.ds(k0, size), :] = jnp.where(eq & (rank > need), -jnp.inf, s)
            return rank[size - 1:size, :]

        for_chunks(body, jnp.zeros((1, TQ), F32))
        return 0

    lax.cond(code % 2 == 1, tie_fix, lambda _: 0, 0)

    def logits(k0, size, mx):
        kc = kka_ref[0, pl.ds(k0, size), :]
        bias = jnp.where(sc_ref[pl.ds(k0, size), :] >= thr, 0.0, -jnp.inf)
        cols = []
        for h in range(ATT_HEADS):
            sh = _bdot(kc, qs_ref[0, :, h * TQ:(h + 1) * TQ]) + bias
            s_ref[pl.ds(k0, size), h * TQ:(h + 1) * TQ] = sh
            cols.append(fold(sh, jnp.maximum))
        return jnp.maximum(mx, jnp.concatenate(cols, axis=1))

    mx = for_chunks(logits, jnp.full((8, W), -jnp.inf, F32))
    mx = jnp.max(mx, axis=0, keepdims=True)

    def weighted(k0, size, carry):
        l, acc = carry
        vc = vt_ref[0, :, pl.ds(k0, size)]
        cols, pv = [], []
        for h in range(ATT_HEADS):
            hs = slice(h * TQ, (h + 1) * TQ)
            p = jnp.exp2(s_ref[pl.ds(k0, size), hs] - mx[:, hs])
            cols.append(fold(p, jnp.add))
            pv.append(_bdot(vc, p.astype(BF16)))
        return l + jnp.concatenate(cols, axis=1), acc + jnp.concatenate(pv, axis=1)

    l, acc = for_chunks(weighted, (jnp.zeros((8, W), F32), jnp.zeros((ATT_HEAD_DIM, W), F32)))
    out_t = acc / jnp.sum(l, axis=0, keepdims=True)
    out_t = jnp.concatenate([out_t[:, h * TQ:(h + 1) * TQ] for h in range(ATT_HEADS)], axis=0)
    o_ref[0] = out_t.T.astype(o_ref.dtype)


def _attn_call(qs, kka, kki, vt, gt, B, S):
    k_sel = min(TOPK_MAX, S // 4)
    W = ATT_HEADS * TQ
    nblk = S // TQ
    kspec = pl.BlockSpec((1, S, LANES), lambda b, j: (b, 0, 0))
    return pl.pallas_call(
        functools.partial(_attn_kernel, k_sel),
        grid=(B, nblk),
        in_specs=[pl.BlockSpec((1, LANES, W), lambda b, j: (b * nblk + j, 0, 0)), kspec, kspec,
                  pl.BlockSpec((1, ATT_HEAD_DIM, S), lambda b, j: (b, 0, 0)),
                  pl.BlockSpec((1, 16, TQ), lambda b, j: (b, 0, j))],
        out_specs=pl.BlockSpec((1, TQ, ATT_W), lambda b, j: (b, j, 0)),
        out_shape=jax.ShapeDtypeStruct((B, S, ATT_W), BF16),
        scratch_shapes=[pltpu.VMEM((S, TQ), F32), pltpu.VMEM((S, W), F32)],
        compiler_params=pltpu.CompilerParams(dimension_semantics=("arbitrary", "arbitrary"),
                                             vmem_limit_bytes=VMEM_LIMIT),
        name="attn",
    )(qs, kka, kki, vt, gt)


def _segment_scan(x, op, identity, seg):
    lane = lax.broadcasted_iota(I32, x.shape, 1) % seg
    s = 1
    while s < seg:
        x = op(x, jnp.where(lane >= s, pltpu.roll(x, s, 1), identity))
        s *= 2
    return x


def _split3(x):
    hi = x.astype(BF16)
    r = x - hi.astype(F32)
    mid = r.astype(BF16)
    return hi, mid, (r - mid.astype(F32)).astype(BF16)


def _mlstm_kernel(nchunks, nb, mq_ref, mk_ref, mv_ref, og_ref, gt_ref, ng_ref, y_ref, c_ref, g_ref):
    L, H, Dh = ML_L, ML_HEADS, ML_HEAD_DIM
    c_ref[...] = jnp.zeros_like(c_ref)
    r_i = lax.broadcasted_iota(I32, (L, L), 0)
    c_i = lax.broadcasted_iota(I32, (L, L), 1)
    eye = (r_i == c_i).astype(BF16)
    eye3 = jnp.concatenate([eye, eye, eye], axis=1)
    tril = r_i >= c_i
    ones_b = jnp.ones((L, LANES), BF16)
    ones2 = jnp.ones((2 * Dh, LANES), BF16)
    heads = range(H)

    for bi in range(nb):
        gi = gt_ref[bi, IDX_HEADS:IDX_HEADS + H, :]
        lf = gt_ref[bi, IDX_HEADS + H:IDX_HEADS + 2 * H, :]
        b_seq = _segment_scan(lf, jnp.add, 0.0, L)
        a_seq = gi - b_seq
        g_ref[bi] = jnp.concatenate([b_seq, a_seq, _segment_scan(a_seq, jnp.maximum, -jnp.inf, L),
                                     jnp.zeros_like(b_seq)], axis=0)

    def lane_sum(x):
        hi = x.astype(BF16)
        return _bdot(jnp.concatenate([hi, (x - hi.astype(F32)).astype(BF16)], axis=1), ones2)

    def one_chunk(bi, c, m_prev):
        t0 = pl.multiple_of(c * L, L)
        rows = pl.ds(t0, L)
        stacked = g_ref[bi, :, rows]
        a_all = stacked[H:2 * H]
        m_last = jnp.maximum(stacked[2 * H:3 * H, L - 1:L], m_prev)
        m_next = stacked[0:H, L - 1:L] + m_last
        cols = _dot_nt(eye3, jnp.concatenate(_split3(stacked), axis=1))
        b_col = [cols[:, h:h + 1] for h in heads]
        a_col = [cols[:, H + h:H + h + 1] for h in heads]
        mcol = [jnp.maximum(cols[:, 2 * H + h:2 * H + h + 1], m_prev[h:h + 1]) for h in heads]
        hsl = [slice(h * Dh, (h + 1) * Dh) for h in heads]
        q = [mq_ref[bi, rows, hsl[h]] for h in heads]
        k = [mk_ref[bi, rows, hsl[h]] for h in heads]
        v_aug = [jnp.concatenate([mv_ref[bi, rows, hsl[h]], ones_b], axis=1) for h in heads]
        cst = [c_ref[bi * H + h] for h in heads]
        qk = [_dot_nt(q[h], k[h]) for h in heads]
        qc = [_bdot(q[h], cst[h].astype(BF16)) for h in heads]
        w_intra = [jnp.where(tril, jnp.exp(a_all[h:h + 1] - mcol[h]), 0.0) for h in heads]
        sv = [_bdot((qk[h] * w_intra[h]).astype(BF16), v_aug[h]) for h in heads]
        for h in heads:
            wv = (v_aug[h].astype(F32) * jnp.exp(a_col[h] - m_last[h:h + 1])).astype(BF16)
            c_ref[bi * H + h] = jnp.exp(m_prev[h:h + 1] - m_last[h:h + 1]) * cst[h] + _dot_tn(k[h], wv)
        haug = [jnp.exp(m_prev[h:h + 1] - mcol[h]) * qc[h] + sv[h] for h in heads]
        hm = [haug[h][:, :Dh] / jnp.maximum(jnp.abs(haug[h][:, Dh:]), jnp.exp(-(b_col[h] + mcol[h]))) for h in heads]
        d = [hm[h] - lane_sum(hm[h]) * (1.0 / Dh) for h in heads]
        var = [lane_sum(d[h] * d[h]) * (1.0 / Dh) for h in heads]
        for h in heads:
            yn = d[h] * lax.rsqrt(var[h] + LN_EPS) * ng_ref[:, hsl[h]]
            y_ref[bi, rows, hsl[h]] = (og_ref[bi, rows, hsl[h]].astype(F32) * yn).astype(y_ref.dtype)
        return m_next

    def chunk(c, m_prev):
        return tuple(one_chunk(bi, c, m_prev[bi]) for bi in range(nb))

    lax.fori_loop(0, nchunks, chunk, tuple(jnp.zeros((H, 1), F32) for _ in range(nb)))


def _mlstm_call(mq, mk, mv, og, gt, ng, B, S):
    nb = 1
    seq = pl.BlockSpec((nb, S, ML_W), lambda b: (b, 0, 0))
    return pl.pallas_call(
        functools.partial(_mlstm_kernel, S // ML_L, nb),
        grid=(B // nb,),
        in_specs=[seq, seq, seq, seq, pl.BlockSpec((nb, 16, S), lambda b: (b, 0, 0)), _const_spec(ng.shape)],
        out_specs=seq,
        out_shape=jax.ShapeDtypeStruct((B, S, ML_W), BF16),
        scratch_shapes=[pltpu.VMEM((nb * ML_HEADS, ML_HEAD_DIM, 2 * ML_HEAD_DIM), F32),
                        pltpu.VMEM((nb, 4 * ML_HEADS, S), F32)],
        compiler_params=pltpu.CompilerParams(dimension_semantics=("arbitrary",), vmem_limit_bytes=VMEM_LIMIT),
        name="mlstm",
    )(mq, mk, mv, og, gt, ng)


def _merge_kernel(x_ref, ya_ref, yb_ref, wg_ref, wua_ref, wub_ref, wo_ref, g_ref, b_ref, h_ref):
    x = x_ref[...]
    xb = x.astype(BF16)
    ma = jax.nn.sigmoid(_bdot(xb, wg_ref[:, :D_MODEL])) * _bdot(ya_ref[...], wua_ref[...])
    mb = jax.nn.sigmoid(_bdot(xb, wg_ref[:, D_MODEL:])) * _bdot(yb_ref[...], wub_ref[...])
    r = DEEPNORM_ALPHA * x + _bdot((ma + mb).astype(BF16), wo_ref[...])
    h_ref[...] = _layer_norm(r, g_ref[...], b_ref[...])


def _merge_call(x2, ya, yb, wg, wua, wub, wo, g, b):
    N = x2.shape[0]
    row = lambda w: pl.BlockSpec((TM, w), lambda i: (i, 0))
    return pl.pallas_call(
        _merge_kernel,
        grid=(N // TM,),
        in_specs=[row(D_MODEL), row(ATT_W), row(ML_W)] + [_const_spec(a.shape) for a in (wg, wua, wub, wo, g, b)],
        out_specs=row(D_MODEL),
        out_shape=jax.ShapeDtypeStruct((N, D_MODEL), F32),
        compiler_params=pltpu.CompilerParams(dimension_semantics=("arbitrary",), vmem_limit_bytes=VMEM_LIMIT),
        name="merge",
    )(x2, ya, yb, wg, wua, wub, wo, g, b)


def _ffn_kernel(h_ref, p_ref, w1_ref, w2_ref, wpg_ref, wpp_ref, g_ref, b_ref, o_ref):
    h = h_ref[...]
    hb = h.astype(BF16)
    ff = jnp.zeros_like(h)
    for c in range(D_FF // D_MODEL):
        cs = slice(c * D_MODEL, (c + 1) * D_MODEL)
        a = jnp.maximum(_bdot(hb, w1_ref[:, cs]), 0.0)
        ff = ff + _bdot((a * a).astype(BF16), w2_ref[cs, :])
    r = DEEPNORM_ALPHA * h + ff
    r = r + jax.nn.sigmoid(_bdot(r.astype(BF16), wpg_ref[...])) * _bdot(p_ref[...].astype(BF16), wpp_ref[...])
    o_ref[...] = _layer_norm(r, g_ref[...], b_ref[...])


def _ffn_call(h1, p2, w1, w2, wpg, wpp, g, b):
    N = h1.shape[0]
    row = lambda w: pl.BlockSpec((TM_FFN, w), lambda i: (i, 0))
    return pl.pallas_call(
        _ffn_kernel,
        grid=(N // TM_FFN,),
        in_specs=[row(D_MODEL), row(PLE_DIM)] + [_const_spec(a.shape) for a in (w1, w2, wpg, wpp, g, b)],
        out_specs=row(D_MODEL),
        out_shape=jax.ShapeDtypeStruct((N, D_MODEL), F32),
        compiler_params=pltpu.CompilerParams(dimension_semantics=("arbitrary",), vmem_limit_bytes=VMEM_LIMIT),
        name="ffn",
    )(h1, p2, w1, w2, wpg, wpp, g, b)


def _split_w_in(w):
    parts, off = {}, 0
    for name, width in _SPLIT:
        parts[name] = w[:, off:off + width]
        off += width
    return parts


def kernel(x, p, positions, w_in, conv_w, conv_b, b_igate, b_fgate, ml_norm_g, w_up_a, w_up_b, w_out,
           ln1_g, ln1_b, w_ff1, w_ff2, w_ple_gate, w_ple_proj, ln2_g, ln2_b):
    B, S, D = x.shape
    assert D == D_MODEL and S % TM == 0 and S % KC == 0 and w_in.shape[0] == DEPTH == 1
    N = B * S
    x2 = x.reshape(N, D)
    pos3 = positions.reshape(N // TM, 1, TM)
    inv_freq = 1.0 / (ROPE_THETA ** (jnp.arange(0, ATT_HEAD_DIM, 2, dtype=F32) / ATT_HEAD_DIM))
    invf = inv_freq.reshape(ATT_HEAD_DIM // 2, 1)

    w = _split_w_in(w_in[0])
    wt = jnp.concatenate([w['att_q'] * (ATT_HEAD_DIM ** -0.5 * LOG2E), w['idx_q'], w['att_k'], w['idx_k'],
                          w['att_v'], w['idx_w'], w['ml_i'], w['ml_f']], axis=1).astype(BF16).T
    wn = jnp.concatenate([w['ml_q'], w['ml_k'], w['ml_v'], w['ml_o']], axis=1).astype(BF16)
    gb = jnp.concatenate([jnp.zeros((IDX_HEADS,), F32), b_igate[0], b_fgate[0]]).reshape(16, 1)
    wg = jnp.concatenate([w['gate_a'], w['gate_b']], axis=1).astype(BF16)

    qs, kka, kki, mq, mk, mv, og, vt, gt = _proj_call(x2, pos3, invf, wt, wn, conv_w[0], conv_b[0].reshape(1, -1),
                                                      gb, B, S)
    r3 = lambda a: a.reshape(B, S, a.shape[-1])
    ya = _attn_call(qs, r3(kka), r3(kki), vt, gt, B, S)
    yb = _mlstm_call(r3(mq), r3(mk), r3(mv), r3(og), gt, ml_norm_g[0].reshape(1, -1), B, S)
    h1 = _merge_call(x2, ya.reshape(N, ATT_W), yb.reshape(N, ML_W), wg, w_up_a[0].astype(BF16),
                     w_up_b[0].astype(BF16), w_out[0].astype(BF16), ln1_g[0].reshape(1, -1), ln1_b[0].reshape(1, -1))
    out = _ffn_call(h1, p[0].reshape(N, PLE_DIM), w_ff1[0].astype(BF16), w_ff2[0].astype(BF16),
                    w_ple_gate[0].astype(BF16), w_ple_proj[0].astype(BF16),
                    ln2_g[0].reshape(1, -1), ln2_b[0].reshape(1, -1))
    return out.reshape(B, S, D)
```

```python
import functools

import jax
import jax.numpy as jnp
import numpy as np
from jax import lax
from jax.experimental import pallas as pl
from jax.experimental.pallas import tpu as pltpu

F32 = jnp.float32
BF16 = jnp.bfloat16
I32 = jnp.int32

D_MODEL = 1024
ATT_HEADS = 8
ATT_HEAD_DIM = 64
IDX_HEADS = 8
IDX_DIM = 64
TOPK_MAX = 256
ML_HEADS = 4
ML_HEAD_DIM = 128
CONV_WIDTH = 4
D_FF = 4 * D_MODEL
PLE_DIM = 256
ROPE_THETA = 10000.0
LN_EPS = 1e-5
DEPTH = 1
DEEPNORM_ALPHA = (2.0 * DEPTH) ** 0.25
IDX_W_SCALE = (IDX_HEADS ** -0.5) * (IDX_DIM ** -0.5)
LOG2E = 1.4426950408889634

ATT_W = ATT_HEADS * ATT_HEAD_DIM
IDX_QW = IDX_HEADS * IDX_DIM
ML_W = ML_HEADS * ML_HEAD_DIM
assert ATT_HEAD_DIM == IDX_DIM == 64 and ATT_HEADS == IDX_HEADS == 8

_SPLIT = (
    ('att_q', ATT_W), ('att_k', ATT_HEAD_DIM), ('att_v', ATT_HEAD_DIM),
    ('idx_q', IDX_QW), ('idx_k', IDX_DIM), ('idx_w', IDX_HEADS),
    ('ml_q', ML_W), ('ml_k', ML_W), ('ml_v', ML_W),
    ('ml_i', ML_HEADS), ('ml_f', ML_HEADS), ('ml_o', ML_W),
    ('gate_a', D_MODEL), ('gate_b', D_MODEL),
)

LANES = 128
VMEM_LIMIT = 56 * 1024 * 1024

TM = 1024
TM_FFN = 1024
TQ = 256
KC = 512
MAX_REFINE = 160
assert KC == 2 * TQ
N_BISECT = 15
ML_L = 256


def _const_spec(shape):
    nd = len(shape)
    return pl.BlockSpec(shape, lambda *_: (0,) * nd, pipeline_mode=pl.Buffered(1))


def _bdot(a, b):
    return jnp.dot(a, b, preferred_element_type=F32)


def _dot_nt(a, b):
    return lax.dot_general(a, b, (((1,), (1,)), ((), ())), preferred_element_type=F32)


def _dot_tn(a, b):
    return lax.dot_general(a, b, (((0,), (0,)), ((), ())), preferred_element_type=F32)


def _layer_norm(r, g, b):
    mu = jnp.mean(r, axis=-1, keepdims=True)
    d = r - mu
    var = jnp.mean(d * d, axis=-1, keepdims=True)
    return d * lax.rsqrt(var + LN_EPS) * g + b


def _proj_kernel(tiles_per_seq, x_ref, pos_ref, invf_ref, wt_ref, wn_ref, cw_ref, cb_ref, gb_ref,
                 qs_ref, kka_ref, kki_ref, mq_ref, mk_ref, mv_ref, og_ref, vt_ref, gt_ref, ext_ref, halo_ref):
    i = pl.program_id(0)
    xb = x_ref[...].astype(BF16)
    HD, HALF = ATT_HEAD_DIM, ATT_HEAD_DIM // 2
    t = _dot_nt(wt_ref[...], xb)
    ang = invf_ref[...] * pos_ref[0].astype(F32)
    cos, sin = jnp.cos(ang), jnp.sin(ang)

    def rope(blk):
        x1, x2 = t[blk * HD: blk * HD + HALF], t[blk * HD + HALF: (blk + 1) * HD]
        return x1 * cos - x2 * sin, x2 * cos + x1 * sin

    for h in range(ATT_HEADS):
        pieces = rope(h) + rope(ATT_HEADS + h)
        for g in range(TM // TQ):
            for r, piece in enumerate(pieces):
                qs_ref[g, r * HALF:(r + 1) * HALF, h * TQ:(h + 1) * TQ] = piece[:, g * TQ:(g + 1) * TQ].astype(BF16)
    kt = jnp.concatenate(rope(2 * ATT_HEADS) + rope(2 * ATT_HEADS + 1), axis=0).T
    is_att = lax.broadcasted_iota(I32, (1, LANES), 1) < HD
    kka_ref[...] = jnp.where(is_att, kt, 0.0).astype(BF16)
    kki_ref[...] = jnp.where(is_att, 0.0, kt).astype(BF16)

    c0 = 0
    not_first = i % tiles_per_seq != 0

    @pl.when(i == 0)
    def _():
        halo_ref[...] = jnp.zeros(halo_ref.shape, F32)

    for g in range(2 * ML_W // LANES // 2):
        cs = slice(c0 + g * 256, c0 + (g + 1) * 256)
        ls = slice(g * 256, (g + 1) * 256)
        pre = _bdot(xb, wn_ref[:, cs])
        ext_ref[0:8, :] = jnp.where(not_first, halo_ref[g], 0.0)
        ext_ref[8:, :] = pre
        halo_ref[g] = pre[TM - 8:, :]
        y = cb_ref[:, ls] + cw_ref[CONV_WIDTH - 1: CONV_WIDTH, ls] * pre
        for k in range(1, CONV_WIDTH):
            y = y + cw_ref[CONV_WIDTH - 1 - k: CONV_WIDTH - k, ls] * ext_ref[8 - k: 8 - k + TM, :]
        y = y * jax.nn.sigmoid(y)
        if g < 2:
            mq_ref[:, g * 256:(g + 1) * 256] = y.astype(mq_ref.dtype)
        else:
            mk_ref[:, (g - 2) * 256:(g - 1) * 256] = (y * (ML_HEAD_DIM ** -0.5)).astype(mk_ref.dtype)
    c1 = c0 + 2 * ML_W
    mv_ref[...] = _bdot(xb, wn_ref[:, c1: c1 + ML_W]).astype(mv_ref.dtype)
    og_ref[...] = jax.nn.sigmoid(_bdot(xb, wn_ref[:, c1 + ML_W: c1 + 2 * ML_W])).astype(og_ref.dtype)

    r0 = (2 * ATT_HEADS + 2) * HD
    vt_ref[0] = t[r0: r0 + HD].astype(vt_ref.dtype)
    g = t[r0 + HD:] + gb_ref[...]
    row = lax.broadcasted_iota(I32, (16, 1), 0)
    gt_ref[0] = jnp.where(row < IDX_HEADS, g * IDX_W_SCALE,
                          jnp.where(row < IDX_HEADS + ML_HEADS, g,
                                    jnp.minimum(g, 0.0) - jnp.log1p(jnp.exp(-jnp.abs(g)))))


def _proj_call(x2, pos3, invf, wt, wn, cw, cb, gb, B, S):
    N = x2.shape[0]
    tps = S // TM
    W = ATT_HEADS * TQ
    row = lambda w: pl.BlockSpec((TM, w), lambda i: (i, 0))
    out_shape = [jax.ShapeDtypeStruct((N // TQ, LANES, W), BF16), jax.ShapeDtypeStruct((N, LANES), BF16),
                 jax.ShapeDtypeStruct((N, LANES), BF16)] + [jax.ShapeDtypeStruct((N, ML_W), BF16)] * 4 + [
        jax.ShapeDtypeStruct((B, ATT_HEAD_DIM, S), BF16), jax.ShapeDtypeStruct((B, 16, S), F32)]
    out_specs = [pl.BlockSpec((TM // TQ, LANES, W), lambda i: (i, 0, 0)), row(LANES), row(LANES)] + [row(ML_W)] * 4 + [
        pl.BlockSpec((1, ATT_HEAD_DIM, TM), lambda i: (i // tps, 0, i % tps)),
        pl.BlockSpec((1, 16, TM), lambda i: (i // tps, 0, i % tps))]
    return pl.pallas_call(
        functools.partial(_proj_kernel, tps),
        grid=(N // TM,),
        in_specs=[row(D_MODEL), pl.BlockSpec((1, 1, TM), lambda i: (i, 0, 0)), _const_spec(invf.shape),
                  _const_spec(wt.shape), _const_spec(wn.shape),
                  _const_spec(cw.shape), _const_spec(cb.shape), _const_spec(gb.shape)],
        out_specs=out_specs, out_shape=out_shape,
        scratch_shapes=[pltpu.VMEM((TM + 8, 256), F32), pltpu.VMEM((2 * ML_W // 256, 8, 256), F32)],
        compiler_params=pltpu.CompilerParams(dimension_semantics=("arbitrary",), vmem_limit_bytes=VMEM_LIMIT),
        name="proj",
    )(x2, pos3, invf, wt, wn, cw, cb, gb)


def _attn_kernel(k_sel, qs_ref, kka_ref, kki_ref, vt_ref, gq_ref, o_ref, sc_ref, s_ref):
    j = pl.program_id(1)
    nkeys = (j + 1) * TQ
    nfull = nkeys // KC
    qpos = j * TQ + lax.broadcasted_iota(I32, (1, TQ), 1)
    wq = gq_ref[0, 0:IDX_HEADS, :]
    W = ATT_HEADS * TQ

    def for_chunks(body, carry):
        carry = lax.fori_loop(0, nfull, lambda c, cr: body(pl.multiple_of(c * KC, KC), KC, cr), carry)
        return lax.cond(nkeys % KC != 0, lambda cr: body(pl.multiple_of(nfull * KC, KC), TQ, cr), lambda cr: cr, carry)

    def fold(a, op):
        out = None
        for r0 in range(0, a.shape[0], 64):
            parts = [a[r0 + r * 8: r0 + (r + 1) * 8] for r in range(8)]
            while len(parts) > 1:
                parts = [op(parts[i], parts[i + 1]) for i in range(0, len(parts), 2)]
            out = parts[0] if out is None else op(out, parts[0])
        return out

    def phase_a(k0, size, carry):
        smin, smax = carry
        kc = kki_ref[0, pl.ds(k0, size), :]
        sc = jnp.zeros((size, TQ), F32)
        for h in range(IDX_HEADS):
            sc = sc + jnp.maximum(_bdot(kc, qs_ref[0, :, h * TQ:(h + 1) * TQ]), 0.0) * wq[h:h + 1, :]
        causal = lax.broadcasted_iota(I32, (size, TQ), 0) + k0 <= qpos
        sc_ref[pl.ds(k0, size), :] = jnp.where(causal, sc, -jnp.inf)
        smin = jnp.minimum(smin, fold(jnp.where(causal, sc, jnp.inf), jnp.minimum))
        smax = jnp.maximum(smax, fold(jnp.where(causal, sc, -jnp.inf), jnp.maximum))
        return smin, smax

    smin, smax = for_chunks(phase_a, (jnp.full((8, TQ), jnp.inf, F32), jnp.full((8, TQ), -jnp.inf, F32)))

    SUB = 64

    def count(pred):
        def body(k0, size, acc):
            for r0 in range(0, size, SUB):
                acc = acc + fold(pred(sc_ref[pl.ds(k0 + r0, SUB), :]).astype(I32), jnp.add)
            return acc
        return jnp.sum(for_chunks(body, jnp.zeros((8, TQ), I32)), axis=0, keepdims=True)

    def max_below(bound):
        def body(k0, size, acc):
            for r0 in range(0, size, SUB):
                s = sc_ref[pl.ds(k0 + r0, SUB), :]
                acc = jnp.maximum(acc, fold(jnp.where(s < bound, s, -jnp.inf), jnp.maximum))
            return acc
        return jnp.max(for_chunks(body, jnp.full((8, TQ), -jnp.inf, F32)), axis=0, keepdims=True)

    kq = jnp.minimum(qpos + 1, k_sel)

    def bisect(_, st):
        lo, hi, hi_ok, clo = st
        x = 0.5 * lo + 0.5 * hi
        c = count(lambda s: s >= x)
        ge = c >= kq
        return jnp.where(ge, x, lo), jnp.where(ge, hi, x), jnp.where(ge, hi_ok, 1), jnp.where(ge, c, clo)

    st = (jnp.min(smin, axis=0, keepdims=True), jnp.max(smax, axis=0, keepdims=True),
          jnp.zeros((1, TQ), I32), qpos + 1)
    st = lax.fori_loop(0, N_BISECT, bisect, st)

    def exact_check(st, tie):
        lo, hi, hi_ok, clo = st
        rem = (clo != kq) & (tie == 0)
        v = max_below(jnp.where(hi_ok > 0, hi, jnp.inf))
        cv = count(lambda s: s >= v)
        hit = rem & (cv >= kq)
        miss = rem & (cv < kq)
        st = (jnp.where(hit, v, lo), jnp.where(miss, v, hi), jnp.where(miss, 1, hi_ok), jnp.where(hit, kq, clo))
        return st, jnp.where(hit & (cv > kq), 1, tie)

    def flags(st, tie):
        return jnp.max(jnp.where((st[3] != kq) & (tie == 0), 2, 0) + tie)

    st, tie = exact_check(st, jnp.zeros((1, TQ), I32))

    def refine(carry):
        st, tie, _, trips = carry
        st, tie = exact_check(lax.fori_loop(0, 2, bisect, st), tie)
        return st, tie, flags(st, tie), trips + 1

    st, tie, code, _ = lax.while_loop(lambda c: (c[2] >= 2) & (c[3] < MAX_REFINE), refine,
                                      (st, tie, flags(st, tie), 0))
    thr = st[0]

    def tie_fix(_):
        need = (kq - count(lambda s: s > thr)).astype(F32)

        def body(k0, size, base):
            tri = lax.broadcasted_iota(I32, (size, size), 0) >= lax.broadcasted_iota(I32, (size, size), 1)
            s = sc_ref[pl.ds(k0, size), :]
            eq = (s == thr) & (tie > 0)
            rank = _bdot(tri.astype(BF16), eq.astype(BF16)) + base
            sc_ref[pl.ds(k0, size), :] = jnp.where(eq & (rank > need), -jnp.inf, s)
            return rank[size - 1:size, :]

        for_chunks(body, jnp.zeros((1, TQ), F32))
        return 0

    lax.cond(code % 2 == 1, tie_fix, lambda _: 0, 0)

    def logits(k0, size, mx):
        kc = kka_ref[0, pl.ds(k0, size), :]
        bias = jnp.where(sc_ref[pl.ds(k0, size), :] >= thr, 0.0, -jnp.inf)
        cols = []
        for h in range(ATT_HEADS):
            sh = _bdot(kc, qs_ref[0, :, h * TQ:(h + 1) * TQ]) + bias
            s_ref[pl.ds(k0, size), h * TQ:(h + 1) * TQ] = sh
            cols.append(fold(sh, jnp.maximum))
        return jnp.maximum(mx, jnp.concatenate(cols, axis=1))

    mx = for_chunks(logits, jnp.full((8, W), -jnp.inf, F32))
    mx = jnp.max(mx, axis=0, keepdims=True)

    def weighted(k0, size, carry):
        l, acc = carry
        vc = vt_ref[0, :, pl.ds(k0, size)]
        cols, pv = [], []
        for h in range(ATT_HEADS):
            hs = slice(h * TQ, (h + 1) * TQ)
            p = jnp.exp2(s_ref[pl.ds(k0, size), hs] - mx[:, hs])
            cols.append(fold(p, jnp.add))
            pv.append(_bdot(vc, p.astype(BF16)))
        return l + jnp.concatenate(cols, axis=1), acc + jnp.concatenate(pv, axis=1)

    l, acc = for_chunks(weighted, (jnp.zeros((8, W), F32), jnp.zeros((ATT_HEAD_DIM, W), F32)))
    out_t = acc / jnp.sum(l, axis=0, keepdims=True)
    out_t = jnp.concatenate([out_t[:, h * TQ:(h + 1) * TQ] for h in range(ATT_HEADS)], axis=0)
    o_ref[0] = out_t.T.astype(o_ref.dtype)


def _attn_call(qs, kka, kki, vt, gt, B, S):
    k_sel = min(TOPK_MAX, S // 4)
    W = ATT_HEADS * TQ
    nblk = S // TQ
    kspec = pl.BlockSpec((1, S, LANES), lambda b, j: (b, 0, 0))
    return pl.pallas_call(
        functools.partial(_attn_kernel, k_sel),
        grid=(B, nblk),
        in_specs=[pl.BlockSpec((1, LANES, W), lambda b, j: (b * nblk + j, 0, 0)), kspec, kspec,
                  pl.BlockSpec((1, ATT_HEAD_DIM, S), lambda b, j: (b, 0, 0)),
                  pl.BlockSpec((1, 16, TQ), lambda b, j: (b, 0, j))],
        out_specs=pl.BlockSpec((1, TQ, ATT_W), lambda b, j: (b, j, 0)),
        out_shape=jax.ShapeDtypeStruct((B, S, ATT_W), BF16),
        scratch_shapes=[pltpu.VMEM((S, TQ), F32), pltpu.VMEM((S, W), F32)],
        compiler_params=pltpu.CompilerParams(dimension_semantics=("arbitrary", "arbitrary"),
                                             vmem_limit_bytes=VMEM_LIMIT),
        name="attn",
    )(qs, kka, kki, vt, gt)


def _segment_scan(x, op, identity, seg):
    lane = lax.broadcasted_iota(I32, x.shape, 1) % seg
    s = 1
    while s < seg:
        x = op(x, jnp.where(lane >= s, pltpu.roll(x, s, 1), identity))
        s *= 2
    return x


def _split3(x):
    hi = x.astype(BF16)
    r = x - hi.astype(F32)
    mid = r.astype(BF16)
    return hi, mid, (r - mid.astype(F32)).astype(BF16)


def _mlstm_kernel(nchunks, nb, mq_ref, mk_ref, mv_ref, og_ref, gt_ref, ng_ref, y_ref, c_ref, g_ref):
    L, H, Dh = ML_L, ML_HEADS, ML_HEAD_DIM
    c_ref[...] = jnp.zeros_like(c_ref)
    r_i = lax.broadcasted_iota(I32, (L, L), 0)
    c_i = lax.broadcasted_iota(I32, (L, L), 1)
    eye = (r_i == c_i).astype(BF16)
    eye3 = jnp.concatenate([eye, eye, eye], axis=1)
    tril = r_i >= c_i
    ones_b = jnp.ones((L, LANES), BF16)
    ones2 = jnp.ones((2 * Dh, LANES), BF16)
    heads = range(H)

    for bi in range(nb):
        gi = gt_ref[bi, IDX_HEADS:IDX_HEADS + H, :]
        lf = gt_ref[bi, IDX_HEADS + H:IDX_HEADS + 2 * H, :]
        b_seq = _segment_scan(lf, jnp.add, 0.0, L)
        a_seq = gi - b_seq
        g_ref[bi] = jnp.concatenate([b_seq, a_seq, _segment_scan(a_seq, jnp.maximum, -jnp.inf, L),
                                     jnp.zeros_like(b_seq)], axis=0)

    def lane_sum(x):
        hi = x.astype(BF16)
        return _bdot(jnp.concatenate([hi, (x - hi.astype(F32)).astype(BF16)], axis=1), ones2)

    def one_chunk(bi, c, m_prev):
        t0 = pl.multiple_of(c * L, L)
        rows = pl.ds(t0, L)
        stacked = g_ref[bi, :, rows]
        a_all = stacked[H:2 * H]
        m_last = jnp.maximum(stacked[2 * H:3 * H, L - 1:L], m_prev)
        m_next = stacked[0:H, L - 1:L] + m_last
        cols = _dot_nt(eye3, jnp.concatenate(_split3(stacked), axis=1))
        b_col = [cols[:, h:h + 1] for h in heads]
        a_col = [cols[:, H + h:H + h + 1] for h in heads]
        mcol = [jnp.maximum(cols[:, 2 * H + h:2 * H + h + 1], m_prev[h:h + 1]) for h in heads]
        hsl = [slice(h * Dh, (h + 1) * Dh) for h in heads]
        q = [mq_ref[bi, rows, hsl[h]] for h in heads]
        k = [mk_ref[bi, rows, hsl[h]] for h in heads]
        v_aug = [jnp.concatenate([mv_ref[bi, rows, hsl[h]], ones_b], axis=1) for h in heads]
        cst = [c_ref[bi * H + h] for h in heads]
        qk = [_dot_nt(q[h], k[h]) for h in heads]
        qc = [_bdot(q[h], cst[h].astype(BF16)) for h in heads]
        w_intra = [jnp.where(tril, jnp.exp(a_all[h:h + 1] - mcol[h]), 0.0) for h in heads]
        sv = [_bdot((qk[h] * w_intra[h]).astype(BF16), v_aug[h]) for h in heads]
        for h in heads:
            wv = (v_aug[h].astype(F32) * jnp.exp(a_col[h] - m_last[h:h + 1])).astype(BF16)
            c_ref[bi * H + h] = jnp.exp(m_prev[h:h + 1] - m_last[h:h + 1]) * cst[h] + _dot_tn(k[h], wv)
        haug = [jnp.exp(m_prev[h:h + 1] - mcol[h]) * qc[h] + sv[h] for h in heads]
        hm = [haug[h][:, :Dh] / jnp.maximum(jnp.abs(haug[h][:, Dh:]), jnp.exp(-(b_col[h] + mcol[h]))) for h in heads]
        d = [hm[h] - lane_sum(hm[h]) * (1.0 / Dh) for h in heads]
        var = [lane_sum(d[h] * d[h]) * (1.0 / Dh) for h in heads]
        for h in heads:
            yn = d[h] * lax.rsqrt(var[h] + LN_EPS) * ng_ref[:, hsl[h]]
            y_ref[bi, rows, hsl[h]] = (og_ref[bi, rows, hsl[h]].astype(F32) * yn).astype(y_ref.dtype)
        return m_next

    def chunk(c, m_prev):
        return tuple(one_chunk(bi, c, m_prev[bi]) for bi in range(nb))

    lax.fori_loop(0, nchunks, chunk, tuple(jnp.zeros((H, 1), F32) for _ in range(nb)))


def _mlstm_call(mq, mk, mv, og, gt, ng, B, S):
    nb = 1
    seq = pl.BlockSpec((nb, S, ML_W), lambda b: (b, 0, 0))
    return pl.pallas_call(
        functools.partial(_mlstm_kernel, S // ML_L, nb),
        grid=(B // nb,),
        in_specs=[seq, seq, seq, seq, pl.BlockSpec((nb, 16, S), lambda b: (b, 0, 0)), _const_spec(ng.shape)],
        out_specs=seq,
        out_shape=jax.ShapeDtypeStruct((B, S, ML_W), BF16),
        scratch_shapes=[pltpu.VMEM((nb * ML_HEADS, ML_HEAD_DIM, 2 * ML_HEAD_DIM), F32),
                        pltpu.VMEM((nb, 4 * ML_HEADS, S), F32)],
        compiler_params=pltpu.CompilerParams(dimension_semantics=("arbitrary",), vmem_limit_bytes=VMEM_LIMIT),
        name="mlstm",
    )(mq, mk, mv, og, gt, ng)


def _merge_kernel(x_ref, ya_ref, yb_ref, wg_ref, wua_ref, wub_ref, wo_ref, g_ref, b_ref, h_ref):
    x = x_ref[...]
    xb = x.astype(BF16)
    ma = jax.nn.sigmoid(_bdot(xb, wg_ref[:, :D_MODEL])) * _bdot(ya_ref[...], wua_ref[...])
    mb = jax.nn.sigmoid(_bdot(xb, wg_ref[:, D_MODEL:])) * _bdot(yb_ref[...], wub_ref[...])
    r = DEEPNORM_ALPHA * x + _bdot((ma + mb).astype(BF16), wo_ref[...])
    h_ref[...] = _layer_norm(r, g_ref[...], b_ref[...])


def _merge_call(x2, ya, yb, wg, wua, wub, wo, g, b):
    N = x2.shape[0]
    row = lambda w: pl.BlockSpec((TM_FFN, w), lambda i: (i, 0))
    return pl.pallas_call(
        _merge_kernel,
        grid=(N // TM_FFN,),
        in_specs=[row(D_MODEL), row(ATT_W), row(ML_W)] + [_const_spec(a.shape) for a in (wg, wua, wub, wo, g, b)],
        out_specs=row(D_MODEL),
        out_shape=jax.ShapeDtypeStruct((N, D_MODEL), F32),
        compiler_params=pltpu.CompilerParams(dimension_semantics=("arbitrary",), vmem_limit_bytes=VMEM_LIMIT),
        name="merge",
    )(x2, ya, yb, wg, wua, wub, wo, g, b)


def _ffn_kernel(h_ref, p_ref, w1_ref, w2_ref, wpg_ref, wpp_ref, g_ref, b_ref, o_ref):
    h = h_ref[...]
    hb = h.astype(BF16)
    ff = jnp.zeros_like(h)
    for c in range(D_FF // D_MODEL):
        cs = slice(c * D_MODEL, (c + 1) * D_MODEL)
        a = jnp.maximum(_bdot(hb, w1_ref[:, cs]), 0.0)
        ff = ff + _bdot((a * a).astype(BF16), w2_ref[cs, :])
    r = DEEPNORM_ALPHA * h + ff
    r = r + jax.nn.sigmoid(_bdot(r.astype(BF16), wpg_ref[...])) * _bdot(p_ref[...].astype(BF16), wpp_ref[...])
    o_ref[...] = _layer_norm(r, g_ref[...], b_ref[...])


def _ffn_call(h1, p2, w1, w2, wpg, wpp, g, b):
    N = h1.shape[0]
    row = lambda w: pl.BlockSpec((TM_FFN, w), lambda i: (i, 0))
    return pl.pallas_call(
        _ffn_kernel,
        grid=(N // TM_FFN,),
        in_specs=[row(D_MODEL), row(PLE_DIM)] + [_const_spec(a.shape) for a in (w1, w2, wpg, wpp, g, b)],
        out_specs=row(D_MODEL),
        out_shape=jax.ShapeDtypeStruct((N, D_MODEL), F32),
        compiler_params=pltpu.CompilerParams(dimension_semantics=("arbitrary",), vmem_limit_bytes=VMEM_LIMIT),
        name="ffn",
    )(h1, p2, w1, w2, wpg, wpp, g, b)


def _split_w_in_t(wt):
    parts, off = {}, 0
    for name, width in _SPLIT:
        parts[name] = wt[off:off + width]
        off += width
    return parts


def kernel(x, p, positions, w_in, conv_w, conv_b, b_igate, b_fgate, ml_norm_g, w_up_a, w_up_b, w_out,
           ln1_g, ln1_b, w_ff1, w_ff2, w_ple_gate, w_ple_proj, ln2_g, ln2_b):
    B, S, D = x.shape
    assert D == D_MODEL and S % TM == 0 and S % KC == 0 and w_in.shape[0] == DEPTH == 1
    N = B * S
    x2 = x.reshape(N, D)
    pos3 = positions.reshape(N // TM, 1, TM)
    inv_freq = 1.0 / (ROPE_THETA ** (jnp.arange(0, ATT_HEAD_DIM, 2, dtype=F32) / ATT_HEAD_DIM))
    invf = inv_freq.reshape(ATT_HEAD_DIM // 2, 1)

    w = _split_w_in_t(w_in[0].T)
    wt = jnp.concatenate([w['att_q'] * (ATT_HEAD_DIM ** -0.5 * LOG2E), w['idx_q'], w['att_k'], w['idx_k'],
                          w['att_v'], w['idx_w'], w['ml_i'], w['ml_f']], axis=0).astype(BF16)
    wn = jnp.concatenate([w['ml_q'], w['ml_k'], w['ml_v'], w['ml_o']], axis=0).astype(BF16).T
    gb = jnp.concatenate([jnp.zeros((IDX_HEADS,), F32), b_igate[0], b_fgate[0]]).reshape(16, 1)
    wg = jnp.concatenate([w['gate_a'], w['gate_b']], axis=0).astype(BF16).T

    qs, kka, kki, mq, mk, mv, og, vt, gt = _proj_call(x2, pos3, invf, wt, wn, conv_w[0], conv_b[0].reshape(1, -1),
                                                      gb, B, S)
    r3 = lambda a: a.reshape(B, S, a.shape[-1])
    ya = _attn_call(qs, r3(kka), r3(kki), vt, gt, B, S)
    yb = _mlstm_call(r3(mq), r3(mk), r3(mv), r3(og), gt, ml_norm_g[0].reshape(1, -1), B, S)
    h1 = _merge_call(x2, ya.reshape(N, ATT_W), yb.reshape(N, ML_W), wg, w_up_a[0].astype(BF16),
                     w_up_b[0].astype(BF16), w_out[0].astype(BF16), ln1_g[0].reshape(1, -1), ln1_b[0].reshape(1, -1))
    out = _ffn_call(h1, p[0].reshape(N, PLE_DIM), w_ff1[0].astype(BF16), w_ff2[0].astype(BF16),
                    w_ple_gate[0].astype(BF16), w_ple_proj[0].astype(BF16),
                    ln2_g[0].reshape(1, -1), ln2_b[0].reshape(1, -1))
    return out.reshape(B, S, D)
```

```python
import functools

import jax
import jax.numpy as jnp
import numpy as np
from jax import lax
from jax.experimental import pallas as pl
from jax.experimental.pallas import tpu as pltpu

F32 = jnp.float32
BF16 = jnp.bfloat16
I32 = jnp.int32

D_MODEL = 1024
ATT_HEADS = 8
ATT_HEAD_DIM = 64
IDX_HEADS = 8
IDX_DIM = 64
TOPK_MAX = 256
ML_HEADS = 4
ML_HEAD_DIM = 128
CONV_WIDTH = 4
D_FF = 4 * D_MODEL
PLE_DIM = 256
ROPE_THETA = 10000.0
LN_EPS = 1e-5
DEPTH = 1
DEEPNORM_ALPHA = (2.0 * DEPTH) ** 0.25
IDX_W_SCALE = (IDX_HEADS ** -0.5) * (IDX_DIM ** -0.5)
LOG2E = 1.4426950408889634

ATT_W = ATT_HEADS * ATT_HEAD_DIM
IDX_QW = IDX_HEADS * IDX_DIM
ML_W = ML_HEADS * ML_HEAD_DIM
assert ATT_HEAD_DIM == IDX_DIM == 64 and ATT_HEADS == IDX_HEADS == 8

_SPLIT = (
    ('att_q', ATT_W), ('att_k', ATT_HEAD_DIM), ('att_v', ATT_HEAD_DIM),
    ('idx_q', IDX_QW), ('idx_k', IDX_DIM), ('idx_w', IDX_HEADS),
    ('ml_q', ML_W), ('ml_k', ML_W), ('ml_v', ML_W),
    ('ml_i', ML_HEADS), ('ml_f', ML_HEADS), ('ml_o', ML_W),
    ('gate_a', D_MODEL), ('gate_b', D_MODEL),
)

LANES = 128
VMEM_LIMIT = 56 * 1024 * 1024

TM = 1024
TM_FFN = 1024
TQ = 256
KC = 1024
MAX_REFINE = 160
assert KC == 4 * TQ and KC & (KC - 1) == 0
N_BISECT = 15
ML_L = 256


def _const_spec(shape):
    nd = len(shape)
    return pl.BlockSpec(shape, lambda *_: (0,) * nd, pipeline_mode=pl.Buffered(1))


def _bdot(a, b):
    return jnp.dot(a, b, preferred_element_type=F32)


def _dot_nt(a, b):
    return lax.dot_general(a, b, (((1,), (1,)), ((), ())), preferred_element_type=F32)


def _dot_tn(a, b):
    return lax.dot_general(a, b, (((0,), (0,)), ((), ())), preferred_element_type=F32)


def _layer_norm(r, g, b):
    mu = jnp.mean(r, axis=-1, keepdims=True)
    d = r - mu
    var = jnp.mean(d * d, axis=-1, keepdims=True)
    return d * lax.rsqrt(var + LN_EPS) * g + b


def _proj_kernel(tiles_per_seq, x_ref, pos_ref, invf_ref, wt_ref, wn_ref, cw_ref, cb_ref, gb_ref,
                 qs_ref, kka_ref, kki_ref, mq_ref, mk_ref, mv_ref, og_ref, vt_ref, gt_ref, ext_ref, halo_ref):
    i = pl.program_id(0)
    xb = x_ref[...].astype(BF16)
    HD, HALF = ATT_HEAD_DIM, ATT_HEAD_DIM // 2
    t = _dot_nt(wt_ref[...], xb)
    ang = invf_ref[...] * pos_ref[0].astype(F32)
    cos, sin = jnp.cos(ang), jnp.sin(ang)

    def rope(blk):
        x1, x2 = t[blk * HD: blk * HD + HALF], t[blk * HD + HALF: (blk + 1) * HD]
        return x1 * cos - x2 * sin, x2 * cos + x1 * sin

    for h in range(ATT_HEADS):
        pieces = rope(h) + rope(ATT_HEADS + h)
        for g in range(TM // TQ):
            for r, piece in enumerate(pieces):
                qs_ref[g, r * HALF:(r + 1) * HALF, h * TQ:(h + 1) * TQ] = piece[:, g * TQ:(g + 1) * TQ].astype(BF16)
    kt = jnp.concatenate(rope(2 * ATT_HEADS) + rope(2 * ATT_HEADS + 1), axis=0).T
    is_att = lax.broadcasted_iota(I32, (1, LANES), 1) < HD
    kka_ref[...] = jnp.where(is_att, kt, 0.0).astype(BF16)
    kki_ref[...] = jnp.where(is_att, 0.0, kt).astype(BF16)

    c0 = 0
    not_first = i % tiles_per_seq != 0

    @pl.when(i == 0)
    def _():
        halo_ref[...] = jnp.zeros(halo_ref.shape, F32)

    for g in range(2 * ML_W // LANES // 2):
        cs = slice(c0 + g * 256, c0 + (g + 1) * 256)
        ls = slice(g * 256, (g + 1) * 256)
        pre = _bdot(xb, wn_ref[:, cs])
        ext_ref[0:8, :] = jnp.where(not_first, halo_ref[g], 0.0)
        ext_ref[8:, :] = pre
        halo_ref[g] = pre[TM - 8:, :]
        y = cb_ref[:, ls] + cw_ref[CONV_WIDTH - 1: CONV_WIDTH, ls] * pre
        for k in range(1, CONV_WIDTH):
            y = y + cw_ref[CONV_WIDTH - 1 - k: CONV_WIDTH - k, ls] * ext_ref[8 - k: 8 - k + TM, :]
        y = y * jax.nn.sigmoid(y)
        if g < 2:
            mq_ref[:, g * 256:(g + 1) * 256] = y.astype(mq_ref.dtype)
        else:
            mk_ref[:, (g - 2) * 256:(g - 1) * 256] = (y * (ML_HEAD_DIM ** -0.5)).astype(mk_ref.dtype)
    c1 = c0 + 2 * ML_W
    mv_ref[...] = _bdot(xb, wn_ref[:, c1: c1 + ML_W]).astype(mv_ref.dtype)
    og_ref[...] = jax.nn.sigmoid(_bdot(xb, wn_ref[:, c1 + ML_W: c1 + 2 * ML_W])).astype(og_ref.dtype)

    r0 = (2 * ATT_HEADS + 2) * HD
    vt_ref[0] = t[r0: r0 + HD].astype(vt_ref.dtype)
    g = t[r0 + HD:] + gb_ref[...]
    row = lax.broadcasted_iota(I32, (16, 1), 0)
    gt_ref[0] = jnp.where(row < IDX_HEADS, g * IDX_W_SCALE,
                          jnp.where(row < IDX_HEADS + ML_HEADS, g,
                                    jnp.minimum(g, 0.0) - jnp.log1p(jnp.exp(-jnp.abs(g)))))


def _proj_call(x2, pos3, invf, wt, wn, cw, cb, gb, B, S):
    N = x2.shape[0]
    tps = S // TM
    W = ATT_HEADS * TQ
    row = lambda w: pl.BlockSpec((TM, w), lambda i: (i, 0))
    out_shape = [jax.ShapeDtypeStruct((N // TQ, LANES, W), BF16), jax.ShapeDtypeStruct((N, LANES), BF16),
                 jax.ShapeDtypeStruct((N, LANES), BF16)] + [jax.ShapeDtypeStruct((N, ML_W), BF16)] * 4 + [
        jax.ShapeDtypeStruct((B, ATT_HEAD_DIM, S), BF16), jax.ShapeDtypeStruct((B, 16, S), F32)]
    out_specs = [pl.BlockSpec((TM // TQ, LANES, W), lambda i: (i, 0, 0)), row(LANES), row(LANES)] + [row(ML_W)] * 4 + [
        pl.BlockSpec((1, ATT_HEAD_DIM, TM), lambda i: (i // tps, 0, i % tps)),
        pl.BlockSpec((1, 16, TM), lambda i: (i // tps, 0, i % tps))]
    return pl.pallas_call(
        functools.partial(_proj_kernel, tps),
        grid=(N // TM,),
        in_specs=[row(D_MODEL), pl.BlockSpec((1, 1, TM), lambda i: (i, 0, 0)), _const_spec(invf.shape),
                  _const_spec(wt.shape), _const_spec(wn.shape),
                  _const_spec(cw.shape), _const_spec(cb.shape), _const_spec(gb.shape)],
        out_specs=out_specs, out_shape=out_shape,
        scratch_shapes=[pltpu.VMEM((TM + 8, 256), F32), pltpu.VMEM((2 * ML_W // 256, 8, 256), F32)],
        compiler_params=pltpu.CompilerParams(dimension_semantics=("arbitrary",), vmem_limit_bytes=VMEM_LIMIT),
        name="proj",
    )(x2, pos3, invf, wt, wn, cw, cb, gb)


def _attn_kernel(k_sel, qs_ref, kka_ref, kki_ref, vt_ref, gq_ref, o_ref, sc_ref, s_ref):
    j = pl.program_id(1)
    nkeys = (j + 1) * TQ
    qpos = j * TQ + lax.broadcasted_iota(I32, (1, TQ), 1)
    wq = gq_ref[0, 0:IDX_HEADS, :]
    W = ATT_HEADS * TQ

    def for_chunks(body, carry):
        carry = lax.fori_loop(0, nkeys // KC, lambda c, cr: body(pl.multiple_of(c * KC, KC), KC, cr), carry)
        k0, size = (nkeys // KC) * KC, KC // 2
        while size >= TQ:
            present = (nkeys & size) != 0
            carry = lax.cond(present, lambda cr, k0=k0, size=size: body(pl.multiple_of(k0, size), size, cr),
                             lambda cr: cr, carry)
            k0, size = k0 + jnp.where(present, size, 0), size // 2
        return carry

    def fold(a, op):
        out = None
        for r0 in range(0, a.shape[0], 64):
            parts = [a[r0 + r * 8: r0 + (r + 1) * 8] for r in range(8)]
            while len(parts) > 1:
                parts = [op(parts[i], parts[i + 1]) for i in range(0, len(parts), 2)]
            out = parts[0] if out is None else op(out, parts[0])
        return out

    def phase_a(k0, size, carry):
        smin, smax = carry
        kc = kki_ref[0, pl.ds(k0, size), :]
        sc = jnp.zeros((size, TQ), F32)
        for h in range(IDX_HEADS):
            sc = sc + jnp.maximum(_bdot(kc, qs_ref[0, :, h * TQ:(h + 1) * TQ]), 0.0) * wq[h:h + 1, :]
        causal = lax.broadcasted_iota(I32, (size, TQ), 0) + k0 <= qpos
        sc_ref[pl.ds(k0, size), :] = jnp.where(causal, sc, -jnp.inf)
        smin = jnp.minimum(smin, fold(jnp.where(causal, sc, jnp.inf), jnp.minimum))
        smax = jnp.maximum(smax, fold(jnp.where(causal, sc, -jnp.inf), jnp.maximum))
        return smin, smax

    smin, smax = for_chunks(phase_a, (jnp.full((8, TQ), jnp.inf, F32), jnp.full((8, TQ), -jnp.inf, F32)))

    SUB = 64

    def count(pred):
        def body(k0, size, acc):
            for r0 in range(0, size, SUB):
                acc = acc + fold(pred(sc_ref[pl.ds(k0 + r0, SUB), :]).astype(I32), jnp.add)
            return acc
        return jnp.sum(for_chunks(body, jnp.zeros((8, TQ), I32)), axis=0, keepdims=True)

    def max_below(bound):
        def body(k0, size, acc):
            for r0 in range(0, size, SUB):
                s = sc_ref[pl.ds(k0 + r0, SUB), :]
                acc = jnp.maximum(acc, fold(jnp.where(s < bound, s, -jnp.inf), jnp.maximum))
            return acc
        return jnp.max(for_chunks(body, jnp.full((8, TQ), -jnp.inf, F32)), axis=0, keepdims=True)

    kq = jnp.minimum(qpos + 1, k_sel)

    def bisect(_, st):
        lo, hi, hi_ok, clo = st
        x = 0.5 * lo + 0.5 * hi
        c = count(lambda s: s >= x)
        ge = c >= kq
        return jnp.where(ge, x, lo), jnp.where(ge, hi, x), jnp.where(ge, hi_ok, 1), jnp.where(ge, c, clo)

    st = (jnp.min(smin, axis=0, keepdims=True), jnp.max(smax, axis=0, keepdims=True),
          jnp.zeros((1, TQ), I32), qpos + 1)
    st = lax.fori_loop(0, N_BISECT, bisect, st)

    def exact_check(st, tie):
        lo, hi, hi_ok, clo = st
        rem = (clo != kq) & (tie == 0)
        v = max_below(jnp.where(hi_ok > 0, hi, jnp.inf))
        cv = count(lambda s: s >= v)
        hit = rem & (cv >= kq)
        miss = rem & (cv < kq)
        st = (jnp.where(hit, v, lo), jnp.where(miss, v, hi), jnp.where(miss, 1, hi_ok), jnp.where(hit, kq, clo))
        return st, jnp.where(hit & (cv > kq), 1, tie)

    def flags(st, tie):
        return jnp.max(jnp.where((st[3] != kq) & (tie == 0), 2, 0) + tie)

    st, tie = exact_check(st, jnp.zeros((1, TQ), I32))

    def refine(carry):
        st, tie, _, trips = carry
        st, tie = exact_check(lax.fori_loop(0, 2, bisect, st), tie)
        return st, tie, flags(st, tie), trips + 1

    st, tie, code, _ = lax.while_loop(lambda c: (c[2] >= 2) & (c[3] < MAX_REFINE), refine,
                                      (st, tie, flags(st, tie), 0))
    thr = st[0]

    def tie_fix(_):
        need = (kq - count(lambda s: s > thr)).astype(F32)

        def body(k0, size, base):
            tri = lax.broadcasted_iota(I32, (size, size), 0) >= lax.broadcasted_iota(I32, (size, size), 1)
            s = sc_ref[pl.ds(k0, size), :]
            eq = (s == thr) & (tie > 0)
            rank = _bdot(tri.astype(BF16), eq.astype(BF16)) + base
            sc_ref[pl.ds(k0, size), :] = jnp.where(eq & (rank > need), -jnp.inf, s)
            return rank[size - 1:size, :]

        for_chunks(body, jnp.zeros((1, TQ), F32))
        return 0

    lax.cond(code % 2 == 1, tie_fix, lambda _: 0, 0)

    def logits(k0, size, mx):
        kc = kka_ref[0, pl.ds(k0, size), :]
        bias = jnp.where(sc_ref[pl.ds(k0, size), :] >= thr, 0.0, -jnp.inf)
        cols = []
        for h in range(ATT_HEADS):
            sh = _bdot(kc, qs_ref[0, :, h * TQ:(h + 1) * TQ]) + bias
            s_ref[pl.ds(k0, size), h * TQ:(h + 1) * TQ] = sh
            cols.append(fold(sh, jnp.maximum))
        return jnp.maximum(mx, jnp.concatenate(cols, axis=1))

    mx = for_chunks(logits, jnp.full((8, W), -jnp.inf, F32))
    mx = jnp.max(mx, axis=0, keepdims=True)

    def weighted(k0, size, carry):
        l, acc = carry
        vc = vt_ref[0, :, pl.ds(k0, size)]
        cols, pv = [], []
        for h in range(ATT_HEADS):
            hs = slice(h * TQ, (h + 1) * TQ)
            p = jnp.exp2(s_ref[pl.ds(k0, size), hs] - mx[:, hs])
            cols.append(fold(p, jnp.add))
            pv.append(_bdot(vc, p.astype(BF16)))
        return l + jnp.concatenate(cols, axis=1), acc + jnp.concatenate(pv, axis=1)

    l, acc = for_chunks(weighted, (jnp.zeros((8, W), F32), jnp.zeros((ATT_HEAD_DIM, W), F32)))
    out_t = acc / jnp.sum(l, axis=0, keepdims=True)
    out_t = jnp.concatenate([out_t[:, h * TQ:(h + 1) * TQ] for h in range(ATT_HEADS)], axis=0)
    o_ref[0] = out_t.T.astype(o_ref.dtype)


def _attn_call(qs, kka, kki, vt, gt, B, S):
    k_sel = min(TOPK_MAX, S // 4)
    W = ATT_HEADS * TQ
    nblk = S // TQ
    kspec = pl.BlockSpec((1, S, LANES), lambda b, j: (b, 0, 0))
    return pl.pallas_call(
        functools.partial(_attn_kernel, k_sel),
        grid=(B, nblk),
        in_specs=[pl.BlockSpec((1, LANES, W), lambda b, j: (b * nblk + j, 0, 0)), kspec, kspec,
                  pl.BlockSpec((1, ATT_HEAD_DIM, S), lambda b, j: (b, 0, 0)),
                  pl.BlockSpec((1, 16, TQ), lambda b, j: (b, 0, j))],
        out_specs=pl.BlockSpec((1, TQ, ATT_W), lambda b, j: (b, j, 0)),
        out_shape=jax.ShapeDtypeStruct((B, S, ATT_W), BF16),
        scratch_shapes=[pltpu.VMEM((S, TQ), F32), pltpu.VMEM((S, W), F32)],
        compiler_params=pltpu.CompilerParams(dimension_semantics=("arbitrary", "arbitrary"),
                                             vmem_limit_bytes=VMEM_LIMIT),
        name="attn",
    )(qs, kka, kki, vt, gt)


def _segment_scan(x, op, identity, seg):
    lane = lax.broadcasted_iota(I32, x.shape, 1) % seg
    s = 1
    while s < seg:
        x = op(x, jnp.where(lane >= s, pltpu.roll(x, s, 1), identity))
        s *= 2
    return x


def _split3(x):
    hi = x.astype(BF16)
    r = x - hi.astype(F32)
    mid = r.astype(BF16)
    return hi, mid, (r - mid.astype(F32)).astype(BF16)


def _mlstm_kernel(nchunks, mq_ref, mk_ref, mv_ref, og_ref, gt_ref, ng_ref, y_ref, c_ref, g_ref):
    L, H, Dh = ML_L, ML_HEADS, ML_HEAD_DIM
    c_ref[...] = jnp.zeros_like(c_ref)
    r_i = lax.broadcasted_iota(I32, (L, L), 0)
    c_i = lax.broadcasted_iota(I32, (L, L), 1)
    eye = (r_i == c_i).astype(BF16)
    eye3 = jnp.concatenate([eye, eye, eye], axis=1)
    tril = r_i >= c_i
    ones_b = jnp.ones((L, LANES), BF16)
    ones2 = jnp.ones((2 * Dh, LANES), BF16)
    heads = range(H)

    gi = gt_ref[0, IDX_HEADS:IDX_HEADS + H, :]
    lf = gt_ref[0, IDX_HEADS + H:IDX_HEADS + 2 * H, :]
    b_seq = _segment_scan(lf, jnp.add, 0.0, L)
    a_seq = gi - b_seq
    g_ref[...] = jnp.concatenate([b_seq, a_seq, _segment_scan(a_seq, jnp.maximum, -jnp.inf, L),
                                  jnp.zeros_like(b_seq)], axis=0)

    def lane_sum(x):
        hi = x.astype(BF16)
        return _bdot(jnp.concatenate([hi, (x - hi.astype(F32)).astype(BF16)], axis=1), ones2)

    def chunk(c, m_prev):
        t0 = pl.multiple_of(c * L, L)
        rows = pl.ds(t0, L)
        stacked = g_ref[:, rows]
        a_all = stacked[H:2 * H]
        m_last = jnp.maximum(stacked[2 * H:3 * H, L - 1:L], m_prev)
        m_next = stacked[0:H, L - 1:L] + m_last
        cols = _dot_nt(eye3, jnp.concatenate(_split3(stacked), axis=1))
        b_col = [cols[:, h:h + 1] for h in heads]
        a_col = [cols[:, H + h:H + h + 1] for h in heads]
        mcol = [jnp.maximum(cols[:, 2 * H + h:2 * H + h + 1], m_prev[h:h + 1]) for h in heads]
        hsl = [slice(h * Dh, (h + 1) * Dh) for h in heads]
        q = [mq_ref[0, rows, hsl[h]] for h in heads]
        k = [mk_ref[0, rows, hsl[h]] for h in heads]
        v_aug = [jnp.concatenate([mv_ref[0, rows, hsl[h]], ones_b], axis=1) for h in heads]
        cst = [c_ref[h] for h in heads]
        qk = [_dot_nt(q[h], k[h]) for h in heads]
        qc = [_bdot(q[h], cst[h].astype(BF16)) for h in heads]
        w_intra = [jnp.where(tril, jnp.exp(a_all[h:h + 1] - mcol[h]), 0.0) for h in heads]
        sv = [_bdot((qk[h] * w_intra[h]).astype(BF16), v_aug[h]) for h in heads]
        for h in heads:
            wv = (v_aug[h].astype(F32) * jnp.exp(a_col[h] - m_last[h:h + 1])).astype(BF16)
            c_ref[h] = jnp.exp(m_prev[h:h + 1] - m_last[h:h + 1]) * cst[h] + _dot_tn(k[h], wv)
        haug = [jnp.exp(m_prev[h:h + 1] - mcol[h]) * qc[h] + sv[h] for h in heads]
        hm = [haug[h][:, :Dh] / jnp.maximum(jnp.abs(haug[h][:, Dh:]), jnp.exp(-(b_col[h] + mcol[h]))) for h in heads]
        d = [hm[h] - lane_sum(hm[h]) * (1.0 / Dh) for h in heads]
        var = [lane_sum(d[h] * d[h]) * (1.0 / Dh) for h in heads]
        for h in heads:
            yn = d[h] * lax.rsqrt(var[h] + LN_EPS) * ng_ref[:, hsl[h]]
            y_ref[0, rows, hsl[h]] = (og_ref[0, rows, hsl[h]].astype(F32) * yn).astype(y_ref.dtype)
        return m_next

    lax.fori_loop(0, nchunks, chunk, jnp.zeros((H, 1), F32), unroll=True)


def _mlstm_call(mq, mk, mv, og, gt, ng, B, S):
    seq = pl.BlockSpec((1, S, ML_W), lambda b: (b, 0, 0))
    return pl.pallas_call(
        functools.partial(_mlstm_kernel, S // ML_L),
        grid=(B,),
        in_specs=[seq, seq, seq, seq, pl.BlockSpec((1, 16, S), lambda b: (b, 0, 0)), _const_spec(ng.shape)],
        out_specs=seq,
        out_shape=jax.ShapeDtypeStruct((B, S, ML_W), BF16),
        scratch_shapes=[pltpu.VMEM((ML_HEADS, ML_HEAD_DIM, 2 * ML_HEAD_DIM), F32),
                        pltpu.VMEM((4 * ML_HEADS, S), F32)],
        compiler_params=pltpu.CompilerParams(dimension_semantics=("arbitrary",), vmem_limit_bytes=VMEM_LIMIT),
        name="mlstm",
    )(mq, mk, mv, og, gt, ng)


def _merge_kernel(x_ref, ya_ref, yb_ref, wg_ref, wua_ref, wub_ref, wo_ref, g_ref, b_ref, h_ref):
    x = x_ref[...]
    xb = x.astype(BF16)
    ma = jax.nn.sigmoid(_bdot(xb, wg_ref[:, :D_MODEL])) * _bdot(ya_ref[...], wua_ref[...])
    mb = jax.nn.sigmoid(_bdot(xb, wg_ref[:, D_MODEL:])) * _bdot(yb_ref[...], wub_ref[...])
    r = DEEPNORM_ALPHA * x + _bdot((ma + mb).astype(BF16), wo_ref[...])
    h_ref[...] = _layer_norm(r, g_ref[...], b_ref[...])


def _merge_call(x2, ya, yb, wg, wua, wub, wo, g, b):
    N = x2.shape[0]
    row = lambda w: pl.BlockSpec((TM_FFN, w), lambda i: (i, 0))
    return pl.pallas_call(
        _merge_kernel,
        grid=(N // TM_FFN,),
        in_specs=[row(D_MODEL), row(ATT_W), row(ML_W)] + [_const_spec(a.shape) for a in (wg, wua, wub, wo, g, b)],
        out_specs=row(D_MODEL),
        out_shape=jax.ShapeDtypeStruct((N, D_MODEL), F32),
        compiler_params=pltpu.CompilerParams(dimension_semantics=("arbitrary",), vmem_limit_bytes=VMEM_LIMIT),
        name="merge",
    )(x2, ya, yb, wg, wua, wub, wo, g, b)


def _ffn_kernel(h_ref, p_ref, w1_ref, w2_ref, wpg_ref, wpp_ref, g_ref, b_ref, o_ref):
    h = h_ref[...]
    hb = h.astype(BF16)
    ff = jnp.zeros_like(h)
    for c in range(D_FF // D_MODEL):
        cs = slice(c * D_MODEL, (c + 1) * D_MODEL)
        a = jnp.maximum(_bdot(hb, w1_ref[:, cs]), 0.0)
        ff = ff + _bdot((a * a).astype(BF16), w2_ref[cs, :])
    r = DEEPNORM_ALPHA * h + ff
    r = r + jax.nn.sigmoid(_bdot(r.astype(BF16), wpg_ref[...])) * _bdot(p_ref[...].astype(BF16), wpp_ref[...])
    o_ref[...] = _layer_norm(r, g_ref[...], b_ref[...])


def _ffn_call(h1, p2, w1, w2, wpg, wpp, g, b):
    N = h1.shape[0]
    row = lambda w: pl.BlockSpec((TM_FFN, w), lambda i: (i, 0))
    return pl.pallas_call(
        _ffn_kernel,
        grid=(N // TM_FFN,),
        in_specs=[row(D_MODEL), row(PLE_DIM)] + [_const_spec(a.shape) for a in (w1, w2, wpg, wpp, g, b)],
        out_specs=row(D_MODEL),
        out_shape=jax.ShapeDtypeStruct((N, D_MODEL), F32),
        compiler_params=pltpu.CompilerParams(dimension_semantics=("arbitrary",), vmem_limit_bytes=VMEM_LIMIT),
        name="ffn",
    )(h1, p2, w1, w2, wpg, wpp, g, b)


def _split_w_in_t(wt):
    parts, off = {}, 0
    for name, width in _SPLIT:
        parts[name] = wt[off:off + width]
        off += width
    return parts


def kernel(x, p, positions, w_in, conv_w, conv_b, b_igate, b_fgate, ml_norm_g, w_up_a, w_up_b, w_out,
           ln1_g, ln1_b, w_ff1, w_ff2, w_ple_gate, w_ple_proj, ln2_g, ln2_b):
    B, S, D = x.shape
    assert D == D_MODEL and S % TM == 0 and S % KC == 0 and w_in.shape[0] == DEPTH == 1
    N = B * S
    x2 = x.reshape(N, D)
    pos3 = positions.reshape(N // TM, 1, TM)
    inv_freq = 1.0 / (ROPE_THETA ** (jnp.arange(0, ATT_HEAD_DIM, 2, dtype=F32) / ATT_HEAD_DIM))
    invf = inv_freq.reshape(ATT_HEAD_DIM // 2, 1)

    w = _split_w_in_t(w_in[0].T)
    wt = jnp.concatenate([w['att_q'] * (ATT_HEAD_DIM ** -0.5 * LOG2E), w['idx_q'], w['att_k'], w['idx_k'],
                          w['att_v'], w['idx_w'], w['ml_i'], w['ml_f']], axis=0).astype(BF16)
    wn = jnp.concatenate([w['ml_q'], w['ml_k'], w['ml_v'], w['ml_o']], axis=0).astype(BF16).T
    gb = jnp.concatenate([jnp.zeros((IDX_HEADS,), F32), b_igate[0], b_fgate[0]]).reshape(16, 1)
    wg = jnp.concatenate([w['gate_a'], w['gate_b']], axis=0).astype(BF16).T

    qs, kka, kki, mq, mk, mv, og, vt, gt = _proj_call(x2, pos3, invf, wt, wn, conv_w[0], conv_b[0].reshape(1, -1),
                                                      gb, B, S)
    r3 = lambda a: a.reshape(B, S, a.shape[-1])
    ya = _attn_call(qs, r3(kka), r3(kki), vt, gt, B, S)
    yb = _mlstm_call(r3(mq), r3(mk), r3(mv), r3(og), gt, ml_norm_g[0].reshape(1, -1), B, S)
    h1 = _merge_call(x2, ya.reshape(N, ATT_W), yb.reshape(N, ML_W), wg, w_up_a[0].astype(BF16),
                     w_up_b[0].astype(BF16), w_out[0].astype(BF16), ln1_g[0].reshape(1, -1), ln1_b[0].reshape(1, -1))
    out = _ffn_call(h1, p[0].reshape(N, PLE_DIM), w_ff1[0].astype(BF16), w_ff2[0].astype(BF16),
                    w_ple_gate[0].astype(BF16), w_ple_proj[0].astype(BF16),
                    ln2_g[0].reshape(1, -1), ln2_b[0].reshape(1, -1))
    return out.reshape(B, S, D)
```

```python
import functools

import jax
import jax.numpy as jnp
import numpy as np
from jax import lax
from jax.experimental import pallas as pl
from jax.experimental.pallas import tpu as pltpu

F32 = jnp.float32
BF16 = jnp.bfloat16
I32 = jnp.int32

D_MODEL = 1024
ATT_HEADS = 8
ATT_HEAD_DIM = 64
IDX_HEADS = 8
IDX_DIM = 64
TOPK_MAX = 256
ML_HEADS = 4
ML_HEAD_DIM = 128
CONV_WIDTH = 4
D_FF = 4 * D_MODEL
PLE_DIM = 256
ROPE_THETA = 10000.0
LN_EPS = 1e-5
DEPTH = 1
DEEPNORM_ALPHA = (2.0 * DEPTH) ** 0.25
IDX_W_SCALE = (IDX_HEADS ** -0.5) * (IDX_DIM ** -0.5)
LOG2E = 1.4426950408889634

ATT_W = ATT_HEADS * ATT_HEAD_DIM
IDX_QW = IDX_HEADS * IDX_DIM
ML_W = ML_HEADS * ML_HEAD_DIM
assert ATT_HEAD_DIM == IDX_DIM == 64 and ATT_HEADS == IDX_HEADS == 8

_SPLIT = (
    ('att_q', ATT_W), ('att_k', ATT_HEAD_DIM), ('att_v', ATT_HEAD_DIM),
    ('idx_q', IDX_QW), ('idx_k', IDX_DIM), ('idx_w', IDX_HEADS),
    ('ml_q', ML_W), ('ml_k', ML_W), ('ml_v', ML_W),
    ('ml_i', ML_HEADS), ('ml_f', ML_HEADS), ('ml_o', ML_W),
    ('gate_a', D_MODEL), ('gate_b', D_MODEL),
)

LANES = 128
VMEM_LIMIT = 56 * 1024 * 1024

TM = 1024
TM_FFN = 1024
TQ = 256
KC = 1024
MAX_REFINE = 160
assert KC == 4 * TQ and KC & (KC - 1) == 0
N_BISECT16 = 10
N_BISECT = 7
ML_L = 256


def _const_spec(shape):
    nd = len(shape)
    return pl.BlockSpec(shape, lambda *_: (0,) * nd, pipeline_mode=pl.Buffered(1))


def _bdot(a, b):
    return jnp.dot(a, b, preferred_element_type=F32)


def _dot_nt(a, b):
    return lax.dot_general(a, b, (((1,), (1,)), ((), ())), preferred_element_type=F32)


def _dot_tn(a, b):
    return lax.dot_general(a, b, (((0,), (0,)), ((), ())), preferred_element_type=F32)


def _layer_norm(r, g, b):
    mu = jnp.mean(r, axis=-1, keepdims=True)
    d = r - mu
    var = jnp.mean(d * d, axis=-1, keepdims=True)
    return d * lax.rsqrt(var + LN_EPS) * g + b


def _proj_kernel(tiles_per_seq, x_ref, pos_ref, invf_ref, wt_ref, wn_ref, cw_ref, cb_ref, gb_ref,
                 qs_ref, kka_ref, kki_ref, mq_ref, mk_ref, mv_ref, og_ref, vt_ref, gt_ref, ext_ref, halo_ref):
    i = pl.program_id(0)
    xb = x_ref[...].astype(BF16)
    HD, HALF = ATT_HEAD_DIM, ATT_HEAD_DIM // 2
    t = _dot_nt(wt_ref[...], xb)
    ang = invf_ref[...] * pos_ref[0].astype(F32)
    cos, sin = jnp.cos(ang), jnp.sin(ang)

    def rope(blk):
        x1, x2 = t[blk * HD: blk * HD + HALF], t[blk * HD + HALF: (blk + 1) * HD]
        return x1 * cos - x2 * sin, x2 * cos + x1 * sin

    for h in range(ATT_HEADS):
        pieces = rope(h) + rope(ATT_HEADS + h)
        for g in range(TM // TQ):
            for r, piece in enumerate(pieces):
                qs_ref[g, r * HALF:(r + 1) * HALF, h * TQ:(h + 1) * TQ] = piece[:, g * TQ:(g + 1) * TQ].astype(BF16)
    kt = jnp.concatenate(rope(2 * ATT_HEADS) + rope(2 * ATT_HEADS + 1), axis=0).T
    is_att = lax.broadcasted_iota(I32, (1, LANES), 1) < HD
    kka_ref[...] = jnp.where(is_att, kt, 0.0).astype(BF16)
    kki_ref[...] = jnp.where(is_att, 0.0, kt).astype(BF16)

    c0 = 0
    not_first = i % tiles_per_seq != 0

    @pl.when(i == 0)
    def _():
        halo_ref[...] = jnp.zeros(halo_ref.shape, F32)

    for g in range(2 * ML_W // LANES // 2):
        cs = slice(c0 + g * 256, c0 + (g + 1) * 256)
        ls = slice(g * 256, (g + 1) * 256)
        pre = _bdot(xb, wn_ref[:, cs])
        ext_ref[0:8, :] = jnp.where(not_first, halo_ref[g], 0.0)
        ext_ref[8:, :] = pre
        halo_ref[g] = pre[TM - 8:, :]
        y = cb_ref[:, ls] + cw_ref[CONV_WIDTH - 1: CONV_WIDTH, ls] * pre
        for k in range(1, CONV_WIDTH):
            y = y + cw_ref[CONV_WIDTH - 1 - k: CONV_WIDTH - k, ls] * ext_ref[8 - k: 8 - k + TM, :]
        y = y * jax.nn.sigmoid(y)
        if g < 2:
            mq_ref[:, g * 256:(g + 1) * 256] = y.astype(mq_ref.dtype)
        else:
            mk_ref[:, (g - 2) * 256:(g - 1) * 256] = (y * (ML_HEAD_DIM ** -0.5)).astype(mk_ref.dtype)
    c1 = c0 + 2 * ML_W
    mv_ref[...] = _bdot(xb, wn_ref[:, c1: c1 + ML_W]).astype(mv_ref.dtype)
    og_ref[...] = jax.nn.sigmoid(_bdot(xb, wn_ref[:, c1 + ML_W: c1 + 2 * ML_W])).astype(og_ref.dtype)

    r0 = (2 * ATT_HEADS + 2) * HD
    vt_ref[0] = t[r0: r0 + HD].astype(vt_ref.dtype)
    g = t[r0 + HD:] + gb_ref[...]
    row = lax.broadcasted_iota(I32, (16, 1), 0)
    gt_ref[0] = jnp.where(row < IDX_HEADS, g * IDX_W_SCALE,
                          jnp.where(row < IDX_HEADS + ML_HEADS, g,
                                    jnp.minimum(g, 0.0) - jnp.log1p(jnp.exp(-jnp.abs(g)))))


def _proj_call(x2, pos3, invf, wt, wn, cw, cb, gb, B, S):
    N = x2.shape[0]
    tps = S // TM
    W = ATT_HEADS * TQ
    row = lambda w: pl.BlockSpec((TM, w), lambda i: (i, 0))
    out_shape = [jax.ShapeDtypeStruct((N // TQ, LANES, W), BF16), jax.ShapeDtypeStruct((N, LANES), BF16),
                 jax.ShapeDtypeStruct((N, LANES), BF16)] + [jax.ShapeDtypeStruct((N, ML_W), BF16)] * 4 + [
        jax.ShapeDtypeStruct((B, ATT_HEAD_DIM, S), BF16), jax.ShapeDtypeStruct((B, 16, S), F32)]
    out_specs = [pl.BlockSpec((TM // TQ, LANES, W), lambda i: (i, 0, 0)), row(LANES), row(LANES)] + [row(ML_W)] * 4 + [
        pl.BlockSpec((1, ATT_HEAD_DIM, TM), lambda i: (i // tps, 0, i % tps)),
        pl.BlockSpec((1, 16, TM), lambda i: (i // tps, 0, i % tps))]
    return pl.pallas_call(
        functools.partial(_proj_kernel, tps),
        grid=(N // TM,),
        in_specs=[row(D_MODEL), pl.BlockSpec((1, 1, TM), lambda i: (i, 0, 0)), _const_spec(invf.shape),
                  _const_spec(wt.shape), _const_spec(wn.shape),
                  _const_spec(cw.shape), _const_spec(cb.shape), _const_spec(gb.shape)],
        out_specs=out_specs, out_shape=out_shape,
        scratch_shapes=[pltpu.VMEM((TM + 8, 256), F32), pltpu.VMEM((2 * ML_W // 256, 8, 256), F32)],
        compiler_params=pltpu.CompilerParams(dimension_semantics=("arbitrary",), vmem_limit_bytes=VMEM_LIMIT),
        name="proj",
    )(x2, pos3, invf, wt, wn, cw, cb, gb)


def _attn_kernel(k_sel, qs_ref, kka_ref, kki_ref, vt_ref, gq_ref, o_ref, sc_ref, scb_ref, s_ref):
    j = pl.program_id(1)
    nkeys = (j + 1) * TQ
    qpos = j * TQ + lax.broadcasted_iota(I32, (1, TQ), 1)
    wq = gq_ref[0, 0:IDX_HEADS, :]
    W = ATT_HEADS * TQ

    def for_chunks(body, carry):
        carry = lax.fori_loop(0, nkeys // KC, lambda c, cr: body(pl.multiple_of(c * KC, KC), KC, cr), carry)
        k0, size = (nkeys // KC) * KC, KC // 2
        while size >= TQ:
            present = (nkeys & size) != 0
            carry = lax.cond(present, lambda cr, k0=k0, size=size: body(pl.multiple_of(k0, size), size, cr),
                             lambda cr: cr, carry)
            k0, size = k0 + jnp.where(present, size, 0), size // 2
        return carry

    def fold(a, op):
        out = None
        for r0 in range(0, a.shape[0], 64):
            parts = [a[r0 + r * 8: r0 + (r + 1) * 8] for r in range(8)]
            while len(parts) > 1:
                parts = [op(parts[i], parts[i + 1]) for i in range(0, len(parts), 2)]
            out = parts[0] if out is None else op(out, parts[0])
        return out

    def phase_a(k0, size, carry):
        smin, smax = carry
        kc = kki_ref[0, pl.ds(k0, size), :]
        sc = jnp.zeros((size, TQ), F32)
        for h in range(IDX_HEADS):
            sc = sc + jnp.maximum(_bdot(kc, qs_ref[0, :, h * TQ:(h + 1) * TQ]), 0.0) * wq[h:h + 1, :]
        causal = lax.broadcasted_iota(I32, (size, TQ), 0) + k0 <= qpos
        sc_ref[pl.ds(k0, size), :] = jnp.where(causal, sc, -jnp.inf)
        scb_ref[pl.ds(k0, size), :] = jnp.where(causal, sc, -jnp.inf).astype(BF16)
        smin = jnp.minimum(smin, fold(jnp.where(causal, sc, jnp.inf), jnp.minimum))
        smax = jnp.maximum(smax, fold(jnp.where(causal, sc, -jnp.inf), jnp.maximum))
        return smin, smax

    smin, smax = for_chunks(phase_a, (jnp.full((8, TQ), jnp.inf, F32), jnp.full((8, TQ), -jnp.inf, F32)))

    SUB = 64

    def count(pred):
        def body(k0, size, acc):
            for r0 in range(0, size, SUB):
                acc = acc + fold(pred(sc_ref[pl.ds(k0 + r0, SUB), :]).astype(I32), jnp.add)
            return acc
        return jnp.sum(for_chunks(body, jnp.zeros((8, TQ), I32)), axis=0, keepdims=True)

    def max_below(bound):
        def body(k0, size, acc):
            for r0 in range(0, size, SUB):
                s = sc_ref[pl.ds(k0 + r0, SUB), :]
                acc = jnp.maximum(acc, fold(jnp.where(s < bound, s, -jnp.inf), jnp.maximum))
            return acc
        return jnp.max(for_chunks(body, jnp.full((8, TQ), -jnp.inf, F32)), axis=0, keepdims=True)

    kq = jnp.minimum(qpos + 1, k_sel)

    def bisect(_, st):
        lo, hi, hi_ok, clo = st
        x = 0.5 * lo + 0.5 * hi
        c = count(lambda s: s >= x)
        ge = c >= kq
        return jnp.where(ge, x, lo), jnp.where(ge, hi, x), jnp.where(ge, hi_ok, 1), jnp.where(ge, c, clo)

    def count16(xb):
        one, zero = jnp.ones((16, TQ), BF16), jnp.zeros((16, TQ), BF16)

        def body(k0, size, acc):
            part = [zero, zero]
            for r0 in range(0, size, 4 * SUB):
                slab = scb_ref[pl.ds(k0 + r0, 4 * SUB), :]
                for n in range(4 * SUB // 16):
                    part[n % 2] = part[n % 2] + jnp.where(slab[n * 16:(n + 1) * 16] >= xb, one, zero)
            return acc + (part[0] + part[1]).astype(F32)
        return jnp.sum(for_chunks(body, jnp.zeros((16, TQ), F32)), axis=0, keepdims=True)

    def bisect16(_, st):
        lo, hi, hi_ok, clo = st
        xb = (0.5 * lo + 0.5 * hi).astype(BF16)
        x = xb.astype(F32)
        ge = count16(xb) >= kq.astype(F32)
        lo_new = jnp.where(ge, jnp.maximum(lo, x - jnp.maximum(jnp.abs(x) * 2.0 ** -7, 1e-30)), lo)
        return (lo_new, jnp.where(ge, hi, jnp.minimum(hi, x)), jnp.where(ge, hi_ok, 1),
                jnp.where(lo_new > lo, -1, clo))

    st = (jnp.min(smin, axis=0, keepdims=True), jnp.max(smax, axis=0, keepdims=True),
          jnp.zeros((1, TQ), I32), qpos + 1)
    st = lax.fori_loop(0, N_BISECT16, bisect16, st)
    st = lax.fori_loop(0, N_BISECT, bisect, st)

    def exact_check(st, tie):
        lo, hi, hi_ok, clo = st
        rem = (clo != kq) & (tie == 0)
        v = max_below(jnp.where(hi_ok > 0, hi, jnp.inf))
        cv = count(lambda s: s >= v)
        hit = rem & (cv >= kq)
        miss = rem & (cv < kq)
        st = (jnp.where(hit, v, lo), jnp.where(miss, v, hi), jnp.where(miss, 1, hi_ok), jnp.where(hit, kq, clo))
        return st, jnp.where(hit & (cv > kq), 1, tie)

    def flags(st, tie):
        return jnp.max(jnp.where((st[3] != kq) & (tie == 0), 2, 0) + tie)

    st, tie = exact_check(st, jnp.zeros((1, TQ), I32))

    def refine(carry):
        st, tie, _, trips = carry
        st, tie = exact_check(lax.fori_loop(0, 2, bisect, st), tie)
        return st, tie, flags(st, tie), trips + 1

    st, tie, code, _ = lax.while_loop(lambda c: (c[2] >= 2) & (c[3] < MAX_REFINE), refine,
                                      (st, tie, flags(st, tie), 0))
    thr = st[0]

    def tie_fix(_):
        need = (kq - count(lambda s: s > thr)).astype(F32)

        def body(k0, size, base):
            tri = lax.broadcasted_iota(I32, (size, size), 0) >= lax.broadcasted_iota(I32, (size, size), 1)
            s = sc_ref[pl.ds(k0, size), :]
            eq = (s == thr) & (tie > 0)
            rank = _bdot(tri.astype(BF16), eq.astype(BF16)) + base
            sc_ref[pl.ds(k0, size), :] = jnp.where(eq & (rank > need), -jnp.inf, s)
            return rank[size - 1:size, :]

        for_chunks(body, jnp.zeros((1, TQ), F32))
        return 0

    lax.cond(code % 2 == 1, tie_fix, lambda _: 0, 0)

    def logits(k0, size, mx):
        kc = kka_ref[0, pl.ds(k0, size), :]
        bias = jnp.where(sc_ref[pl.ds(k0, size), :] >= thr, 0.0, -jnp.inf)
        cols = []
        for h in range(ATT_HEADS):
            sh = _bdot(kc, qs_ref[0, :, h * TQ:(h + 1) * TQ]) + bias
            s_ref[pl.ds(k0, size), h * TQ:(h + 1) * TQ] = sh
            cols.append(fold(sh, jnp.maximum))
        return jnp.maximum(mx, jnp.concatenate(cols, axis=1))

    mx = for_chunks(logits, jnp.full((8, W), -jnp.inf, F32))
    mx = jnp.max(mx, axis=0, keepdims=True)

    def weighted(k0, size, carry):
        l, acc = carry
        vc = vt_ref[0, :, pl.ds(k0, size)]
        cols, pv = [], []
        for h in range(ATT_HEADS):
            hs = slice(h * TQ, (h + 1) * TQ)
            p = jnp.exp2(s_ref[pl.ds(k0, size), hs] - mx[:, hs])
            cols.append(fold(p, jnp.add))
            pv.append(_bdot(vc, p.astype(BF16)))
        return l + jnp.concatenate(cols, axis=1), acc + jnp.concatenate(pv, axis=1)

    l, acc = for_chunks(weighted, (jnp.zeros((8, W), F32), jnp.zeros((ATT_HEAD_DIM, W), F32)))
    out_t = acc / jnp.sum(l, axis=0, keepdims=True)
    out_t = jnp.concatenate([out_t[:, h * TQ:(h + 1) * TQ] for h in range(ATT_HEADS)], axis=0)
    o_ref[0] = out_t.T.astype(o_ref.dtype)


def _attn_call(qs, kka, kki, vt, gt, B, S):
    k_sel = min(TOPK_MAX, S // 4)
    W = ATT_HEADS * TQ
    nblk = S // TQ
    kspec = pl.BlockSpec((1, S, LANES), lambda b, j: (b, 0, 0))
    return pl.pallas_call(
        functools.partial(_attn_kernel, k_sel),
        grid=(B, nblk),
        in_specs=[pl.BlockSpec((1, LANES, W), lambda b, j: (b * nblk + j, 0, 0)), kspec, kspec,
                  pl.BlockSpec((1, ATT_HEAD_DIM, S), lambda b, j: (b, 0, 0)),
                  pl.BlockSpec((1, 16, TQ), lambda b, j: (b, 0, j))],
        out_specs=pl.BlockSpec((1, TQ, ATT_W), lambda b, j: (b, j, 0)),
        out_shape=jax.ShapeDtypeStruct((B, S, ATT_W), BF16),
        scratch_shapes=[pltpu.VMEM((S, TQ), F32), pltpu.VMEM((S, TQ), BF16), pltpu.VMEM((S, W), F32)],
        compiler_params=pltpu.CompilerParams(dimension_semantics=("arbitrary", "arbitrary"),
                                             vmem_limit_bytes=VMEM_LIMIT),
        name="attn",
    )(qs, kka, kki, vt, gt)


def _segment_scan(x, op, identity, seg):
    lane = lax.broadcasted_iota(I32, x.shape, 1) % seg
    s = 1
    while s < seg:
        x = op(x, jnp.where(lane >= s, pltpu.roll(x, s, 1), identity))
        s *= 2
    return x


def _split3(x):
    hi = x.astype(BF16)
    r = x - hi.astype(F32)
    mid = r.astype(BF16)
    return hi, mid, (r - mid.astype(F32)).astype(BF16)


def _mlstm_kernel(nchunks, mq_ref, mk_ref, mv_ref, og_ref, gt_ref, ng_ref, y_ref, c_ref, g_ref):
    L, H, Dh = ML_L, ML_HEADS, ML_HEAD_DIM
    c_ref[...] = jnp.zeros_like(c_ref)
    r_i = lax.broadcasted_iota(I32, (L, L), 0)
    c_i = lax.broadcasted_iota(I32, (L, L), 1)
    eye = (r_i == c_i).astype(BF16)
    eye3 = jnp.concatenate([eye, eye, eye], axis=1)
    tril = r_i >= c_i
    ones_b = jnp.ones((L, LANES), BF16)
    ones2 = jnp.ones((2 * Dh, LANES), BF16)
    heads = range(H)

    gi = gt_ref[0, IDX_HEADS:IDX_HEADS + H, :]
    lf = gt_ref[0, IDX_HEADS + H:IDX_HEADS + 2 * H, :]
    b_seq = _segment_scan(lf, jnp.add, 0.0, L)
    a_seq = gi - b_seq
    g_ref[...] = jnp.concatenate([b_seq, a_seq, _segment_scan(a_seq, jnp.maximum, -jnp.inf, L),
                                  jnp.zeros_like(b_seq)], axis=0)

    def lane_sum(x):
        hi = x.astype(BF16)
        return _bdot(jnp.concatenate([hi, (x - hi.astype(F32)).astype(BF16)], axis=1), ones2)

    def chunk(c, m_prev):
        t0 = pl.multiple_of(c * L, L)
        rows = pl.ds(t0, L)
        stacked = g_ref[:, rows]
        a_all = stacked[H:2 * H]
        m_last = jnp.maximum(stacked[2 * H:3 * H, L - 1:L], m_prev)
        m_next = stacked[0:H, L - 1:L] + m_last
        cols = _dot_nt(eye3, jnp.concatenate(_split3(stacked), axis=1))
        b_col = [cols[:, h:h + 1] for h in heads]
        a_col = [cols[:, H + h:H + h + 1] for h in heads]
        mcol = [jnp.maximum(cols[:, 2 * H + h:2 * H + h + 1], m_prev[h:h + 1]) for h in heads]
        hsl = [slice(h * Dh, (h + 1) * Dh) for h in heads]
        q = [mq_ref[0, rows, hsl[h]] for h in heads]
        k = [mk_ref[0, rows, hsl[h]] for h in heads]
        v_aug = [jnp.concatenate([mv_ref[0, rows, hsl[h]], ones_b], axis=1) for h in heads]
        cst = [c_ref[h] for h in heads]
        qk = [_dot_nt(q[h], k[h]) for h in heads]
        qc = [_bdot(q[h], cst[h].astype(BF16)) for h in heads]
        w_intra = [jnp.where(tril, jnp.exp(a_all[h:h + 1] - mcol[h]), 0.0) for h in heads]
        sv = [_bdot((qk[h] * w_intra[h]).astype(BF16), v_aug[h]) for h in heads]
        for h in heads:
            wv = (v_aug[h].astype(F32) * jnp.exp(a_col[h] - m_last[h:h + 1])).astype(BF16)
            c_ref[h] = jnp.exp(m_prev[h:h + 1] - m_last[h:h + 1]) * cst[h] + _dot_tn(k[h], wv)
        haug = [jnp.exp(m_prev[h:h + 1] - mcol[h]) * qc[h] + sv[h] for h in heads]
        hm = [haug[h][:, :Dh] / jnp.maximum(jnp.abs(haug[h][:, Dh:]), jnp.exp(-(b_col[h] + mcol[h]))) for h in heads]
        d = [hm[h] - lane_sum(hm[h]) * (1.0 / Dh) for h in heads]
        var = [lane_sum(d[h] * d[h]) * (1.0 / Dh) for h in heads]
        for h in heads:
            yn = d[h] * lax.rsqrt(var[h] + LN_EPS) * ng_ref[:, hsl[h]]
            y_ref[0, rows, hsl[h]] = (og_ref[0, rows, hsl[h]].astype(F32) * yn).astype(y_ref.dtype)
        return m_next

    lax.fori_loop(0, nchunks, chunk, jnp.zeros((H, 1), F32), unroll=True)


def _mlstm_call(mq, mk, mv, og, gt, ng, B, S):
    seq = pl.BlockSpec((1, S, ML_W), lambda b: (b, 0, 0))
    return pl.pallas_call(
        functools.partial(_mlstm_kernel, S // ML_L),
        grid=(B,),
        in_specs=[seq, seq, seq, seq, pl.BlockSpec((1, 16, S), lambda b: (b, 0, 0)), _const_spec(ng.shape)],
        out_specs=seq,
        out_shape=jax.ShapeDtypeStruct((B, S, ML_W), BF16),
        scratch_shapes=[pltpu.VMEM((ML_HEADS, ML_HEAD_DIM, 2 * ML_HEAD_DIM), F32),
                        pltpu.VMEM((4 * ML_HEADS, S), F32)],
        compiler_params=pltpu.CompilerParams(dimension_semantics=("arbitrary",), vmem_limit_bytes=VMEM_LIMIT),
        name="mlstm",
    )(mq, mk, mv, og, gt, ng)


def _merge_kernel(x_ref, ya_ref, yb_ref, wg_ref, wua_ref, wub_ref, wo_ref, g_ref, b_ref, h_ref):
    x = x_ref[...]
    xb = x.astype(BF16)
    ma = jax.nn.sigmoid(_bdot(xb, wg_ref[:, :D_MODEL])) * _bdot(ya_ref[...], wua_ref[...])
    mb = jax.nn.sigmoid(_bdot(xb, wg_ref[:, D_MODEL:])) * _bdot(yb_ref[...], wub_ref[...])
    r = DEEPNORM_ALPHA * x + _bdot((ma + mb).astype(BF16), wo_ref[...])
    h_ref[...] = _layer_norm(r, g_ref[...], b_ref[...])


def _merge_call(x2, ya, yb, wg, wua, wub, wo, g, b):
    N = x2.shape[0]
    row = lambda w: pl.BlockSpec((TM_FFN, w), lambda i: (i, 0))
    return pl.pallas_call(
        _merge_kernel,
        grid=(N // TM_FFN,),
        in_specs=[row(D_MODEL), row(ATT_W), row(ML_W)] + [_const_spec(a.shape) for a in (wg, wua, wub, wo, g, b)],
        out_specs=row(D_MODEL),
        out_shape=jax.ShapeDtypeStruct((N, D_MODEL), F32),
        compiler_params=pltpu.CompilerParams(dimension_semantics=("arbitrary",), vmem_limit_bytes=VMEM_LIMIT),
        name="merge",
    )(x2, ya, yb, wg, wua, wub, wo, g, b)


def _ffn_kernel(h_ref, p_ref, w1_ref, w2_ref, wpg_ref, wpp_ref, g_ref, b_ref, o_ref):
    h = h_ref[...]
    hb = h.astype(BF16)
    ff = jnp.zeros_like(h)
    for c in range(D_FF // D_MODEL):
        cs = slice(c * D_MODEL, (c + 1) * D_MODEL)
        a = jnp.maximum(_bdot(hb, w1_ref[:, cs]), 0.0)
        ff = ff + _bdot((a * a).astype(BF16), w2_ref[cs, :])
    r = DEEPNORM_ALPHA * h + ff
    r = r + jax.nn.sigmoid(_bdot(r.astype(BF16), wpg_ref[...])) * _bdot(p_ref[...].astype(BF16), wpp_ref[...])
    o_ref[...] = _layer_norm(r, g_ref[...], b_ref[...])


def _ffn_call(h1, p2, w1, w2, wpg, wpp, g, b):
    N = h1.shape[0]
    row = lambda w: pl.BlockSpec((TM_FFN, w), lambda i: (i, 0))
    return pl.pallas_call(
        _ffn_kernel,
        grid=(N // TM_FFN,),
        in_specs=[row(D_MODEL), row(PLE_DIM)] + [_const_spec(a.shape) for a in (w1, w2, wpg, wpp, g, b)],
        out_specs=row(D_MODEL),
        out_shape=jax.ShapeDtypeStruct((N, D_MODEL), F32),
        compiler_params=pltpu.CompilerParams(dimension_semantics=("arbitrary",), vmem_limit_bytes=VMEM_LIMIT),
        name="ffn",
    )(h1, p2, w1, w2, wpg, wpp, g, b)


def _split_w_in_t(wt):
    parts, off = {}, 0
    for name, width in _SPLIT:
        parts[name] = wt[off:off + width]
        off += width
    return parts


def kernel(x, p, positions, w_in, conv_w, conv_b, b_igate, b_fgate, ml_norm_g, w_up_a, w_up_b, w_out,
           ln1_g, ln1_b, w_ff1, w_ff2, w_ple_gate, w_ple_proj, ln2_g, ln2_b):
    B, S, D = x.shape
    assert D == D_MODEL and S % TM == 0 and S % KC == 0 and w_in.shape[0] == DEPTH == 1
    N = B * S
    x2 = x.reshape(N, D)
    pos3 = positions.reshape(N // TM, 1, TM)
    inv_freq = 1.0 / (ROPE_THETA ** (jnp.arange(0, ATT_HEAD_DIM, 2, dtype=F32) / ATT_HEAD_DIM))
    invf = inv_freq.reshape(ATT_HEAD_DIM // 2, 1)

    w = _split_w_in_t(w_in[0].T)
    wt = jnp.concatenate([w['att_q'] * (ATT_HEAD_DIM ** -0.5 * LOG2E), w['idx_q'], w['att_k'], w['idx_k'],
                          w['att_v'], w['idx_w'], w['ml_i'], w['ml_f']], axis=0).astype(BF16)
    wn = jnp.concatenate([w['ml_q'], w['ml_k'], w['ml_v'], w['ml_o']], axis=0).astype(BF16).T
    gb = jnp.concatenate([jnp.zeros((IDX_HEADS,), F32), b_igate[0], b_fgate[0]]).reshape(16, 1)
    wg = jnp.concatenate([w['gate_a'], w['gate_b']], axis=0).astype(BF16).T

    qs, kka, kki, mq, mk, mv, og, vt, gt = _proj_call(x2, pos3, invf, wt, wn, conv_w[0], conv_b[0].reshape(1, -1),
                                                      gb, B, S)
    r3 = lambda a: a.reshape(B, S, a.shape[-1])
    ya = _attn_call(qs, r3(kka), r3(kki), vt, gt, B, S)
    yb = _mlstm_call(r3(mq), r3(mk), r3(mv), r3(og), gt, ml_norm_g[0].reshape(1, -1), B, S)
    h1 = _merge_call(x2, ya.reshape(N, ATT_W), yb.reshape(N, ML_W), wg, w_up_a[0].astype(BF16),
                     w_up_b[0].astype(BF16), w_out[0].astype(BF16), ln1_g[0].reshape(1, -1), ln1_b[0].reshape(1, -1))
    out = _ffn_call(h1, p[0].reshape(N, PLE_DIM), w_ff1[0].astype(BF16), w_ff2[0].astype(BF16),
                    w_ple_gate[0].astype(BF16), w_ple_proj[0].astype(BF16),
                    ln2_g[0].reshape(1, -1), ln2_b[0].reshape(1, -1))
    return out.reshape(B, S, D)
```

```python
import functools

import jax
import jax.numpy as jnp
import numpy as np
from jax import lax
from jax.experimental import pallas as pl
from jax.experimental.pallas import tpu as pltpu

F32 = jnp.float32
BF16 = jnp.bfloat16
I32 = jnp.int32

D_MODEL = 1024
ATT_HEADS = 8
ATT_HEAD_DIM = 64
IDX_HEADS = 8
IDX_DIM = 64
TOPK_MAX = 256
ML_HEADS = 4
ML_HEAD_DIM = 128
CONV_WIDTH = 4
D_FF = 4 * D_MODEL
PLE_DIM = 256
ROPE_THETA = 10000.0
LN_EPS = 1e-5
DEPTH = 1
DEEPNORM_ALPHA = (2.0 * DEPTH) ** 0.25
IDX_W_SCALE = (IDX_HEADS ** -0.5) * (IDX_DIM ** -0.5)
LOG2E = 1.4426950408889634

ATT_W = ATT_HEADS * ATT_HEAD_DIM
IDX_QW = IDX_HEADS * IDX_DIM
ML_W = ML_HEADS * ML_HEAD_DIM
assert ATT_HEAD_DIM == IDX_DIM == 64 and ATT_HEADS == IDX_HEADS == 8

_SPLIT = (
    ('att_q', ATT_W), ('att_k', ATT_HEAD_DIM), ('att_v', ATT_HEAD_DIM),
    ('idx_q', IDX_QW), ('idx_k', IDX_DIM), ('idx_w', IDX_HEADS),
    ('ml_q', ML_W), ('ml_k', ML_W), ('ml_v', ML_W),
    ('ml_i', ML_HEADS), ('ml_f', ML_HEADS), ('ml_o', ML_W),
    ('gate_a', D_MODEL), ('gate_b', D_MODEL),
)

LANES = 128
VMEM_LIMIT = 56 * 1024 * 1024

TM = 1024
TM_FFN = 1024
CONV_SLAB = 512
TQ = 256
KC = 1024
MAX_REFINE = 160
assert KC == 4 * TQ and KC & (KC - 1) == 0
N_BISECT16 = 10
N_BISECT = 7
ML_L = 256


def _const_spec(shape):
    nd = len(shape)
    return pl.BlockSpec(shape, lambda *_: (0,) * nd, pipeline_mode=pl.Buffered(1))


def _bdot(a, b):
    return jnp.dot(a, b, preferred_element_type=F32)


def _dot_nt(a, b):
    return lax.dot_general(a, b, (((1,), (1,)), ((), ())), preferred_element_type=F32)


def _dot_tn(a, b):
    return lax.dot_general(a, b, (((0,), (0,)), ((), ())), preferred_element_type=F32)


def _layer_norm(r, g, b):
    mu = jnp.mean(r, axis=-1, keepdims=True)
    d = r - mu
    var = jnp.mean(d * d, axis=-1, keepdims=True)
    return d * lax.rsqrt(var + LN_EPS) * g + b


def _proj_kernel(tiles_per_seq, x_ref, pos_ref, invf_ref, wt_ref, wn_ref, cw_ref, cb_ref, gb_ref,
                 qs_ref, kka_ref, kki_ref, mq_ref, mk_ref, mv_ref, og_ref, vt_ref, gt_ref, ext_ref, halo_ref):
    i = pl.program_id(0)
    xb = x_ref[...].astype(BF16)
    HD, HALF = ATT_HEAD_DIM, ATT_HEAD_DIM // 2
    t = _dot_nt(wt_ref[...], xb)
    ang = invf_ref[...] * pos_ref[0].astype(F32)
    cos, sin = jnp.cos(ang), jnp.sin(ang)

    def rope(blk, ts):
        x1, x2 = t[blk * HD: blk * HD + HALF, ts], t[blk * HD + HALF: (blk + 1) * HD, ts]
        return x1 * cos[:, ts] - x2 * sin[:, ts], x2 * cos[:, ts] + x1 * sin[:, ts]

    is_att = lax.broadcasted_iota(I32, (1, LANES), 1) < HD
    for g in range(TM // TQ):
        ts = slice(g * TQ, (g + 1) * TQ)
        for h in range(ATT_HEADS):
            for r, piece in enumerate(rope(h, ts) + rope(ATT_HEADS + h, ts)):
                qs_ref[g, r * HALF:(r + 1) * HALF, h * TQ:(h + 1) * TQ] = piece.astype(BF16)
        kt = jnp.concatenate(rope(2 * ATT_HEADS, ts) + rope(2 * ATT_HEADS + 1, ts), axis=0).T
        kka_ref[ts, :] = jnp.where(is_att, kt, 0.0).astype(BF16)
        kki_ref[ts, :] = jnp.where(is_att, 0.0, kt).astype(BF16)

    c0 = 0
    not_first = i % tiles_per_seq != 0

    @pl.when(i == 0)
    def _():
        halo_ref[...] = jnp.zeros(halo_ref.shape, F32)

    for g in range(2 * ML_W // LANES // 2):
        cs = slice(c0 + g * 256, c0 + (g + 1) * 256)
        ls = slice(g * 256, (g + 1) * 256)
        ext_ref[0:8, :] = jnp.where(not_first, halo_ref[g], 0.0)
        ext_ref[8:, :] = _bdot(xb, wn_ref[:, cs])
        halo_ref[g] = ext_ref[TM:, :]
        for r0 in range(0, TM, CONV_SLAB):
            y = cb_ref[:, ls]
            for k in range(CONV_WIDTH):
                y = y + cw_ref[CONV_WIDTH - 1 - k: CONV_WIDTH - k, ls] * ext_ref[8 - k + r0: 8 - k + r0 + CONV_SLAB, :]
            y = y * jax.nn.sigmoid(y)
            rows = slice(r0, r0 + CONV_SLAB)
            if g < 2:
                mq_ref[rows, g * 256:(g + 1) * 256] = y.astype(mq_ref.dtype)
            else:
                mk_ref[rows, (g - 2) * 256:(g - 1) * 256] = (y * (ML_HEAD_DIM ** -0.5)).astype(mk_ref.dtype)
    c1 = c0 + 2 * ML_W
    mv_ref[...] = _bdot(xb, wn_ref[:, c1: c1 + ML_W]).astype(mv_ref.dtype)
    og_ref[...] = jax.nn.sigmoid(_bdot(xb, wn_ref[:, c1 + ML_W: c1 + 2 * ML_W])).astype(og_ref.dtype)

    r0 = (2 * ATT_HEADS + 2) * HD
    vt_ref[0] = t[r0: r0 + HD].astype(vt_ref.dtype)
    g = t[r0 + HD:] + gb_ref[...]
    row = lax.broadcasted_iota(I32, (16, 1), 0)
    gt_ref[0] = jnp.where(row < IDX_HEADS, g * IDX_W_SCALE,
                          jnp.where(row < IDX_HEADS + ML_HEADS, g,
                                    jnp.minimum(g, 0.0) - jnp.log1p(jnp.exp(-jnp.abs(g)))))


def _proj_call(x2, pos3, invf, wt, wn, cw, cb, gb, B, S):
    N = x2.shape[0]
    tps = S // TM
    W = ATT_HEADS * TQ
    row = lambda w: pl.BlockSpec((TM, w), lambda i: (i, 0))
    out_shape = [jax.ShapeDtypeStruct((N // TQ, LANES, W), BF16), jax.ShapeDtypeStruct((N, LANES), BF16),
                 jax.ShapeDtypeStruct((N, LANES), BF16)] + [jax.ShapeDtypeStruct((N, ML_W), BF16)] * 4 + [
        jax.ShapeDtypeStruct((B, ATT_HEAD_DIM, S), BF16), jax.ShapeDtypeStruct((B, 16, S), F32)]
    out_specs = [pl.BlockSpec((TM // TQ, LANES, W), lambda i: (i, 0, 0)), row(LANES), row(LANES)] + [row(ML_W)] * 4 + [
        pl.BlockSpec((1, ATT_HEAD_DIM, TM), lambda i: (i // tps, 0, i % tps)),
        pl.BlockSpec((1, 16, TM), lambda i: (i // tps, 0, i % tps))]
    return pl.pallas_call(
        functools.partial(_proj_kernel, tps),
        grid=(N // TM,),
        in_specs=[row(D_MODEL), pl.BlockSpec((1, 1, TM), lambda i: (i, 0, 0)), _const_spec(invf.shape),
                  _const_spec(wt.shape), _const_spec(wn.shape),
                  _const_spec(cw.shape), _const_spec(cb.shape), _const_spec(gb.shape)],
        out_specs=out_specs, out_shape=out_shape,
        scratch_shapes=[pltpu.VMEM((TM + 8, 256), F32), pltpu.VMEM((2 * ML_W // 256, 8, 256), F32)],
        compiler_params=pltpu.CompilerParams(dimension_semantics=("arbitrary",), vmem_limit_bytes=VMEM_LIMIT),
        name="proj",
    )(x2, pos3, invf, wt, wn, cw, cb, gb)


def _attn_kernel(k_sel, qs_ref, kka_ref, kki_ref, vt_ref, gq_ref, o_ref, sc_ref, scb_ref, s_ref):
    j = pl.program_id(1)
    nkeys = (j + 1) * TQ
    qpos = j * TQ + lax.broadcasted_iota(I32, (1, TQ), 1)
    wq = gq_ref[0, 0:IDX_HEADS, :]
    W = ATT_HEADS * TQ

    def for_chunks(body, carry):
        carry = lax.fori_loop(0, nkeys // KC, lambda c, cr: body(pl.multiple_of(c * KC, KC), KC, cr), carry)
        k0, size = (nkeys // KC) * KC, KC // 2
        while size >= TQ:
            present = (nkeys & size) != 0
            carry = lax.cond(present, lambda cr, k0=k0, size=size: body(pl.multiple_of(k0, size), size, cr),
                             lambda cr: cr, carry)
            k0, size = k0 + jnp.where(present, size, 0), size // 2
        return carry

    def fold(a, op):
        out = None
        for r0 in range(0, a.shape[0], 64):
            parts = [a[r0 + r * 8: r0 + (r + 1) * 8] for r in range(8)]
            while len(parts) > 1:
                parts = [op(parts[i], parts[i + 1]) for i in range(0, len(parts), 2)]
            out = parts[0] if out is None else op(out, parts[0])
        return out

    def phase_a(k0, size, carry):
        smin, smax = carry
        kc = kki_ref[0, pl.ds(k0, size), :]
        sc = jnp.zeros((size, TQ), F32)
        for h in range(IDX_HEADS):
            sc = sc + jnp.maximum(_bdot(kc, qs_ref[0, :, h * TQ:(h + 1) * TQ]), 0.0) * wq[h:h + 1, :]
        causal = lax.broadcasted_iota(I32, (size, TQ), 0) + k0 <= qpos
        sc_ref[pl.ds(k0, size), :] = jnp.where(causal, sc, -jnp.inf)
        scb_ref[pl.ds(k0, size), :] = jnp.where(causal, sc, -jnp.inf).astype(BF16)
        smin = jnp.minimum(smin, fold(jnp.where(causal, sc, jnp.inf), jnp.minimum))
        smax = jnp.maximum(smax, fold(jnp.where(causal, sc, -jnp.inf), jnp.maximum))
        return smin, smax

    smin, smax = for_chunks(phase_a, (jnp.full((8, TQ), jnp.inf, F32), jnp.full((8, TQ), -jnp.inf, F32)))

    SUB = 64

    def count(pred):
        def body(k0, size, acc):
            for r0 in range(0, size, SUB):
                acc = acc + fold(pred(sc_ref[pl.ds(k0 + r0, SUB), :]).astype(I32), jnp.add)
            return acc
        return jnp.sum(for_chunks(body, jnp.zeros((8, TQ), I32)), axis=0, keepdims=True)

    def max_below(bound):
        def body(k0, size, acc):
            for r0 in range(0, size, SUB):
                s = sc_ref[pl.ds(k0 + r0, SUB), :]
                acc = jnp.maximum(acc, fold(jnp.where(s < bound, s, -jnp.inf), jnp.maximum))
            return acc
        return jnp.max(for_chunks(body, jnp.full((8, TQ), -jnp.inf, F32)), axis=0, keepdims=True)

    kq = jnp.minimum(qpos + 1, k_sel)

    def bisect(_, st):
        lo, hi, hi_ok, clo = st
        x = 0.5 * lo + 0.5 * hi
        c = count(lambda s: s >= x)
        ge = c >= kq
        return jnp.where(ge, x, lo), jnp.where(ge, hi, x), jnp.where(ge, hi_ok, 1), jnp.where(ge, c, clo)

    def count16(xb):
        one, zero = jnp.ones((16, TQ), BF16), jnp.zeros((16, TQ), BF16)

        def body(k0, size, acc):
            part = [zero, zero]
            for r0 in range(0, size, 4 * SUB):
                slab = scb_ref[pl.ds(k0 + r0, 4 * SUB), :]
                for n in range(4 * SUB // 16):
                    part[n % 2] = part[n % 2] + jnp.where(slab[n * 16:(n + 1) * 16] >= xb, one, zero)
            return acc + (part[0] + part[1]).astype(F32)
        return jnp.sum(for_chunks(body, jnp.zeros((16, TQ), F32)), axis=0, keepdims=True)

    def bisect16(_, st):
        lo, hi, hi_ok, clo = st
        xb = (0.5 * lo + 0.5 * hi).astype(BF16)
        x = xb.astype(F32)
        ge = count16(xb) >= kq.astype(F32)
        lo_new = jnp.where(ge, jnp.maximum(lo, x - jnp.maximum(jnp.abs(x) * 2.0 ** -7, 1e-30)), lo)
        return (lo_new, jnp.where(ge, hi, jnp.minimum(hi, x)), jnp.where(ge, hi_ok, 1),
                jnp.where(lo_new > lo, -1, clo))

    st = (jnp.min(smin, axis=0, keepdims=True), jnp.max(smax, axis=0, keepdims=True),
          jnp.zeros((1, TQ), I32), qpos + 1)
    st = lax.fori_loop(0, N_BISECT16, bisect16, st)
    st = lax.fori_loop(0, N_BISECT, bisect, st)

    def exact_check(st, tie):
        lo, hi, hi_ok, clo = st
        rem = (clo != kq) & (tie == 0)
        v = max_below(jnp.where(hi_ok > 0, hi, jnp.inf))
        cv = count(lambda s: s >= v)
        hit = rem & (cv >= kq)
        miss = rem & (cv < kq)
        st = (jnp.where(hit, v, lo), jnp.where(miss, v, hi), jnp.where(miss, 1, hi_ok), jnp.where(hit, kq, clo))
        return st, jnp.where(hit & (cv > kq), 1, tie)

    def flags(st, tie):
        return jnp.max(jnp.where((st[3] != kq) & (tie == 0), 2, 0) + tie)

    st, tie = exact_check(st, jnp.zeros((1, TQ), I32))

    def refine(carry):
        st, tie, _, trips = carry
        st, tie = exact_check(lax.fori_loop(0, 2, bisect, st), tie)
        return st, tie, flags(st, tie), trips + 1

    st, tie, code, _ = lax.while_loop(lambda c: (c[2] >= 2) & (c[3] < MAX_REFINE), refine,
                                      (st, tie, flags(st, tie), 0))
    thr = st[0]

    def tie_fix(_):
        need = (kq - count(lambda s: s > thr)).astype(F32)

        def body(k0, size, base):
            tri = lax.broadcasted_iota(I32, (size, size), 0) >= lax.broadcasted_iota(I32, (size, size), 1)
            s = sc_ref[pl.ds(k0, size), :]
            eq = (s == thr) & (tie > 0)
            rank = _bdot(tri.astype(BF16), eq.astype(BF16)) + base
            sc_ref[pl.ds(k0, size), :] = jnp.where(eq & (rank > need), -jnp.inf, s)
            return rank[size - 1:size, :]

        for_chunks(body, jnp.zeros((1, TQ), F32))
        return 0

    lax.cond(code % 2 == 1, tie_fix, lambda _: 0, 0)

    def logits(k0, size, mx):
        kc = kka_ref[0, pl.ds(k0, size), :]
        bias = jnp.where(sc_ref[pl.ds(k0, size), :] >= thr, 0.0, -jnp.inf)
        cols = []
        for h in range(ATT_HEADS):
            sh = _bdot(kc, qs_ref[0, :, h * TQ:(h + 1) * TQ]) + bias
            s_ref[pl.ds(k0, size), h * TQ:(h + 1) * TQ] = sh
            cols.append(fold(sh, jnp.maximum))
        return jnp.maximum(mx, jnp.concatenate(cols, axis=1))

    mx = for_chunks(logits, jnp.full((8, W), -jnp.inf, F32))
    mx = jnp.max(mx, axis=0, keepdims=True)

    def weighted(k0, size, carry):
        l, acc = carry
        vc = vt_ref[0, :, pl.ds(k0, size)]
        cols, pv = [], []
        for h in range(ATT_HEADS):
            hs = slice(h * TQ, (h + 1) * TQ)
            p = jnp.exp2(s_ref[pl.ds(k0, size), hs] - mx[:, hs])
            cols.append(fold(p, jnp.add))
            pv.append(_bdot(vc, p.astype(BF16)))
        return l + jnp.concatenate(cols, axis=1), acc + jnp.concatenate(pv, axis=1)

    l, acc = for_chunks(weighted, (jnp.zeros((8, W), F32), jnp.zeros((ATT_HEAD_DIM, W), F32)))
    out_t = acc / jnp.sum(l, axis=0, keepdims=True)
    out_t = jnp.concatenate([out_t[:, h * TQ:(h + 1) * TQ] for h in range(ATT_HEADS)], axis=0)
    o_ref[0] = out_t.T.astype(o_ref.dtype)


def _attn_call(qs, kka, kki, vt, gt, B, S):
    k_sel = min(TOPK_MAX, S // 4)
    W = ATT_HEADS * TQ
    nblk = S // TQ
    kspec = pl.BlockSpec((1, S, LANES), lambda b, j: (b, 0, 0))
    return pl.pallas_call(
        functools.partial(_attn_kernel, k_sel),
        grid=(B, nblk),
        in_specs=[pl.BlockSpec((1, LANES, W), lambda b, j: (b * nblk + j, 0, 0)), kspec, kspec,
                  pl.BlockSpec((1, ATT_HEAD_DIM, S), lambda b, j: (b, 0, 0)),
                  pl.BlockSpec((1, 16, TQ), lambda b, j: (b, 0, j))],
        out_specs=pl.BlockSpec((1, TQ, ATT_W), lambda b, j: (b, j, 0)),
        out_shape=jax.ShapeDtypeStruct((B, S, ATT_W), BF16),
        scratch_shapes=[pltpu.VMEM((S, TQ), F32), pltpu.VMEM((S, TQ), BF16), pltpu.VMEM((S, W), F32)],
        compiler_params=pltpu.CompilerParams(dimension_semantics=("arbitrary", "arbitrary"),
                                             vmem_limit_bytes=VMEM_LIMIT),
        name="attn",
    )(qs, kka, kki, vt, gt)


def _segment_scan(x, op, identity, seg):
    lane = lax.broadcasted_iota(I32, x.shape, 1) % seg
    s = 1
    while s < seg:
        x = op(x, jnp.where(lane >= s, pltpu.roll(x, s, 1), identity))
        s *= 2
    return x


def _split3(x):
    hi = x.astype(BF16)
    r = x - hi.astype(F32)
    mid = r.astype(BF16)
    return hi, mid, (r - mid.astype(F32)).astype(BF16)


def _mlstm_kernel(nchunks, mq_ref, mk_ref, mv_ref, og_ref, gt_ref, ng_ref, y_ref, c_ref, g_ref):
    L, H, Dh = ML_L, ML_HEADS, ML_HEAD_DIM
    c_ref[...] = jnp.zeros_like(c_ref)
    r_i = lax.broadcasted_iota(I32, (L, L), 0)
    c_i = lax.broadcasted_iota(I32, (L, L), 1)
    eye = (r_i == c_i).astype(BF16)
    eye3 = jnp.concatenate([eye, eye, eye], axis=1)
    tril = r_i >= c_i
    ones_b = jnp.ones((L, LANES), BF16)
    ones2 = jnp.ones((2 * Dh, LANES), BF16)
    heads = range(H)

    gi = gt_ref[0, IDX_HEADS:IDX_HEADS + H, :]
    lf = gt_ref[0, IDX_HEADS + H:IDX_HEADS + 2 * H, :]
    b_seq = _segment_scan(lf, jnp.add, 0.0, L)
    a_seq = gi - b_seq
    g_ref[...] = jnp.concatenate([b_seq, a_seq, _segment_scan(a_seq, jnp.maximum, -jnp.inf, L),
                                  jnp.zeros_like(b_seq)], axis=0)

    def lane_sum(x):
        hi = x.astype(BF16)
        return _bdot(jnp.concatenate([hi, (x - hi.astype(F32)).astype(BF16)], axis=1), ones2)

    def chunk(c, m_prev):
        t0 = pl.multiple_of(c * L, L)
        rows = pl.ds(t0, L)
        stacked = g_ref[:, rows]
        a_all = stacked[H:2 * H]
        m_last = jnp.maximum(stacked[2 * H:3 * H, L - 1:L], m_prev)
        m_next = stacked[0:H, L - 1:L] + m_last
        cols = _dot_nt(eye3, jnp.concatenate(_split3(stacked), axis=1))
        b_col = [cols[:, h:h + 1] for h in heads]
        a_col = [cols[:, H + h:H + h + 1] for h in heads]
        mcol = [jnp.maximum(cols[:, 2 * H + h:2 * H + h + 1], m_prev[h:h + 1]) for h in heads]
        hsl = [slice(h * Dh, (h + 1) * Dh) for h in heads]
        q = [mq_ref[0, rows, hsl[h]] for h in heads]
        k = [mk_ref[0, rows, hsl[h]] for h in heads]
        v_aug = [jnp.concatenate([mv_ref[0, rows, hsl[h]], ones_b], axis=1) for h in heads]
        cst = [c_ref[h] for h in heads]
        qk = [_dot_nt(q[h], k[h]) for h in heads]
        qc = [_bdot(q[h], cst[h].astype(BF16)) for h in heads]
        w_intra = [jnp.where(tril, jnp.exp(a_all[h:h + 1] - mcol[h]), 0.0) for h in heads]
        sv = [_bdot((qk[h] * w_intra[h]).astype(BF16), v_aug[h]) for h in heads]
        for h in heads:
            wv = (v_aug[h].astype(F32) * jnp.exp(a_col[h] - m_last[h:h + 1])).astype(BF16)
            c_ref[h] = jnp.exp(m_prev[h:h + 1] - m_last[h:h + 1]) * cst[h] + _dot_tn(k[h], wv)
        haug = [jnp.exp(m_prev[h:h + 1] - mcol[h]) * qc[h] + sv[h] for h in heads]
        hm = [haug[h][:, :Dh] / jnp.maximum(jnp.abs(haug[h][:, Dh:]), jnp.exp(-(b_col[h] + mcol[h]))) for h in heads]
        d = [hm[h] - lane_sum(hm[h]) * (1.0 / Dh) for h in heads]
        var = [lane_sum(d[h] * d[h]) * (1.0 / Dh) for h in heads]
        for h in heads:
            yn = d[h] * lax.rsqrt(var[h] + LN_EPS) * ng_ref[:, hsl[h]]
            y_ref[0, rows, hsl[h]] = (og_ref[0, rows, hsl[h]].astype(F32) * yn).astype(y_ref.dtype)
        return m_next

    lax.fori_loop(0, nchunks, chunk, jnp.zeros((H, 1), F32), unroll=True)


def _mlstm_call(mq, mk, mv, og, gt, ng, B, S):
    seq = pl.BlockSpec((1, S, ML_W), lambda b: (b, 0, 0))
    return pl.pallas_call(
        functools.partial(_mlstm_kernel, S // ML_L),
        grid=(B,),
        in_specs=[seq, seq, seq, seq, pl.BlockSpec((1, 16, S), lambda b: (b, 0, 0)), _const_spec(ng.shape)],
        out_specs=seq,
        out_shape=jax.ShapeDtypeStruct((B, S, ML_W), BF16),
        scratch_shapes=[pltpu.VMEM((ML_HEADS, ML_HEAD_DIM, 2 * ML_HEAD_DIM), F32),
                        pltpu.VMEM((4 * ML_HEADS, S), F32)],
        compiler_params=pltpu.CompilerParams(dimension_semantics=("arbitrary",), vmem_limit_bytes=VMEM_LIMIT),
        name="mlstm",
    )(mq, mk, mv, og, gt, ng)


def _merge_kernel(x_ref, ya_ref, yb_ref, wg_ref, wua_ref, wub_ref, wo_ref, g_ref, b_ref, h_ref):
    x = x_ref[...]
    xb = x.astype(BF16)
    ma = jax.nn.sigmoid(_bdot(xb, wg_ref[:, :D_MODEL])) * _bdot(ya_ref[...], wua_ref[...])
    mb = jax.nn.sigmoid(_bdot(xb, wg_ref[:, D_MODEL:])) * _bdot(yb_ref[...], wub_ref[...])
    r = DEEPNORM_ALPHA * x + _bdot((ma + mb).astype(BF16), wo_ref[...])
    h_ref[...] = _layer_norm(r, g_ref[...], b_ref[...])


def _merge_call(x2, ya, yb, wg, wua, wub, wo, g, b):
    N = x2.shape[0]
    row = lambda w: pl.BlockSpec((TM_FFN, w), lambda i: (i, 0))
    return pl.pallas_call(
        _merge_kernel,
        grid=(N // TM_FFN,),
        in_specs=[row(D_MODEL), row(ATT_W), row(ML_W)] + [_const_spec(a.shape) for a in (wg, wua, wub, wo, g, b)],
        out_specs=row(D_MODEL),
        out_shape=jax.ShapeDtypeStruct((N, D_MODEL), F32),
        compiler_params=pltpu.CompilerParams(dimension_semantics=("arbitrary",), vmem_limit_bytes=VMEM_LIMIT),
        name="merge",
    )(x2, ya, yb, wg, wua, wub, wo, g, b)


def _ffn_kernel(h_ref, p_ref, w1_ref, w2_ref, wpg_ref, wpp_ref, g_ref, b_ref, o_ref):
    h = h_ref[...]
    hb = h.astype(BF16)
    ff = jnp.zeros_like(h)
    for c in range(D_FF // D_MODEL):
        cs = slice(c * D_MODEL, (c + 1) * D_MODEL)
        a = jnp.maximum(_bdot(hb, w1_ref[:, cs]), 0.0)
        ff = ff + _bdot((a * a).astype(BF16), w2_ref[cs, :])
    r = DEEPNORM_ALPHA * h + ff
    r = r + jax.nn.sigmoid(_bdot(r.astype(BF16), wpg_ref[...])) * _bdot(p_ref[...].astype(BF16), wpp_ref[...])
    o_ref[...] = _layer_norm(r, g_ref[...], b_ref[...])


def _ffn_call(h1, p2, w1, w2, wpg, wpp, g, b):
    N = h1.shape[0]
    row = lambda w: pl.BlockSpec((TM_FFN, w), lambda i: (i, 0))
    return pl.pallas_call(
        _ffn_kernel,
        grid=(N // TM_FFN,),
        in_specs=[row(D_MODEL), row(PLE_DIM)] + [_const_spec(a.shape) for a in (w1, w2, wpg, wpp, g, b)],
        out_specs=row(D_MODEL),
        out_shape=jax.ShapeDtypeStruct((N, D_MODEL), F32),
        compiler_params=pltpu.CompilerParams(dimension_semantics=("arbitrary",), vmem_limit_bytes=VMEM_LIMIT),
        name="ffn",
    )(h1, p2, w1, w2, wpg, wpp, g, b)


def _split_w_in_t(wt):
    parts, off = {}, 0
    for name, width in _SPLIT:
        parts[name] = wt[off:off + width]
        off += width
    return parts


def kernel(x, p, positions, w_in, conv_w, conv_b, b_igate, b_fgate, ml_norm_g, w_up_a, w_up_b, w_out,
           ln1_g, ln1_b, w_ff1, w_ff2, w_ple_gate, w_ple_proj, ln2_g, ln2_b):
    B, S, D = x.shape
    assert D == D_MODEL and S % TM == 0 and S % KC == 0 and w_in.shape[0] == DEPTH == 1
    N = B * S
    x2 = x.reshape(N, D)
    pos3 = positions.reshape(N // TM, 1, TM)
    inv_freq = 1.0 / (ROPE_THETA ** (jnp.arange(0, ATT_HEAD_DIM, 2, dtype=F32) / ATT_HEAD_DIM))
    invf = inv_freq.reshape(ATT_HEAD_DIM // 2, 1)

    w = _split_w_in_t(w_in[0].T)
    wt = jnp.concatenate([w['att_q'] * (ATT_HEAD_DIM ** -0.5 * LOG2E), w['idx_q'], w['att_k'], w['idx_k'],
                          w['att_v'], w['idx_w'], w['ml_i'], w['ml_f']], axis=0).astype(BF16)
    wn = jnp.concatenate([w['ml_q'], w['ml_k'], w['ml_v'], w['ml_o']], axis=0).astype(BF16).T
    gb = jnp.concatenate([jnp.zeros((IDX_HEADS,), F32), b_igate[0], b_fgate[0]]).reshape(16, 1)
    wg = jnp.concatenate([w['gate_a'], w['gate_b']], axis=0).astype(BF16).T

    qs, kka, kki, mq, mk, mv, og, vt, gt = _proj_call(x2, pos3, invf, wt, wn, conv_w[0], conv_b[0].reshape(1, -1),
                                                      gb, B, S)
    r3 = lambda a: a.reshape(B, S, a.shape[-1])
    ya = _attn_call(qs, r3(kka), r3(kki), vt, gt, B, S)
    yb = _mlstm_call(r3(mq), r3(mk), r3(mv), r3(og), gt, ml_norm_g[0].reshape(1, -1), B, S)
    h1 = _merge_call(x2, ya.reshape(N, ATT_W), yb.reshape(N, ML_W), wg, w_up_a[0].astype(BF16),
                     w_up_b[0].astype(BF16), w_out[0].astype(BF16), ln1_g[0].reshape(1, -1), ln1_b[0].reshape(1, -1))
    out = _ffn_call(h1, p[0].reshape(N, PLE_DIM), w_ff1[0].astype(BF16), w_ff2[0].astype(BF16),
                    w_ple_gate[0].astype(BF16), w_ple_proj[0].astype(BF16),
                    ln2_g[0].reshape(1, -1), ln2_b[0].reshape(1, -1))
    return out.reshape(B, S, D)
```

```python
import functools

import jax
import jax.numpy as jnp
from jax import lax
from jax.experimental import pallas as pl
from jax.experimental.pallas import tpu as pltpu

F32 = jnp.float32
BF16 = jnp.bfloat16
I32 = jnp.int32

D_MODEL = 1024
ATT_HEADS = 8
ATT_HEAD_DIM = 64
IDX_HEADS = 8
IDX_DIM = 64
TOPK_MAX = 256
ML_HEADS = 4
ML_HEAD_DIM = 128
CONV_WIDTH = 4
D_FF = 4 * D_MODEL
PLE_DIM = 256
ROPE_THETA = 10000.0
LN_EPS = 1e-5
DEPTH = 1
DEEPNORM_ALPHA = (2.0 * DEPTH) ** 0.25
IDX_W_SCALE = (IDX_HEADS ** -0.5) * (IDX_DIM ** -0.5)
LOG2E = 1.4426950408889634

ATT_W = ATT_HEADS * ATT_HEAD_DIM
IDX_QW = IDX_HEADS * IDX_DIM
ML_W = ML_HEADS * ML_HEAD_DIM
assert ATT_HEAD_DIM == IDX_DIM == 64 and ATT_HEADS == IDX_HEADS == 8

_SPLIT = (
    ('att_q', ATT_W), ('att_k', ATT_HEAD_DIM), ('att_v', ATT_HEAD_DIM),
    ('idx_q', IDX_QW), ('idx_k', IDX_DIM), ('idx_w', IDX_HEADS),
    ('ml_q', ML_W), ('ml_k', ML_W), ('ml_v', ML_W),
    ('ml_i', ML_HEADS), ('ml_f', ML_HEADS), ('ml_o', ML_W),
    ('gate_a', D_MODEL), ('gate_b', D_MODEL),
)

LANES = 128
V7X_VMEM_BYTES = 64 * 1024 * 1024
VMEM_LIMIT = V7X_VMEM_BYTES - 8 * 1024 * 1024

TM = 1024
TM_FFN = 1024
CONV_SLAB = 512
TQ = 256
Q_BLOCKS = 2
KC = 1024
MAX_REFINE = 160
assert KC == 4 * TQ and KC & (KC - 1) == 0
N_BISECT16 = 10
N_BISECT = 7
ML_L = 256


def _const_spec(shape):
    nd = len(shape)
    return pl.BlockSpec(shape, lambda *_: (0,) * nd, pipeline_mode=pl.Buffered(1))


def _bdot(a, b):
    return jnp.dot(a, b, preferred_element_type=F32)


def _dot_nt(a, b):
    return lax.dot_general(a, b, (((1,), (1,)), ((), ())), preferred_element_type=F32)


def _dot_tn(a, b):
    return lax.dot_general(a, b, (((0,), (0,)), ((), ())), preferred_element_type=F32)


def _layer_norm(r, g, b):
    mu = jnp.mean(r, axis=-1, keepdims=True)
    d = r - mu
    var = jnp.mean(d * d, axis=-1, keepdims=True)
    return d * lax.rsqrt(var + LN_EPS) * g + b


def _proj_kernel(tiles_per_seq, x_ref, pos_ref, invf_ref, wt_ref, wn_ref, cw_ref, cb_ref, gb_ref,
                 qs_ref, kka_ref, kki_ref, mq_ref, mk_ref, mv_ref, og_ref, vt_ref, gt_ref, ext_ref, halo_ref):
    i = pl.program_id(0)
    xb = x_ref[...].astype(BF16)
    HD, HALF = ATT_HEAD_DIM, ATT_HEAD_DIM // 2
    t = _dot_nt(wt_ref[...], xb)
    ang = invf_ref[...] * pos_ref[0].astype(F32)
    cos, sin = jnp.cos(ang), jnp.sin(ang)

    def rope(blk, ts):
        x1, x2 = t[blk * HD: blk * HD + HALF, ts], t[blk * HD + HALF: (blk + 1) * HD, ts]
        return x1 * cos[:, ts] - x2 * sin[:, ts], x2 * cos[:, ts] + x1 * sin[:, ts]

    is_att = lax.broadcasted_iota(I32, (1, LANES), 1) < HD
    for g in range(TM // TQ):
        ts = slice(g * TQ, (g + 1) * TQ)
        for h in range(ATT_HEADS):
            for r, piece in enumerate(rope(h, ts) + rope(ATT_HEADS + h, ts)):
                qs_ref[g, r * HALF:(r + 1) * HALF, h * TQ:(h + 1) * TQ] = piece.astype(BF16)
        kt = jnp.concatenate(rope(2 * ATT_HEADS, ts) + rope(2 * ATT_HEADS + 1, ts), axis=0).T
        kka_ref[ts, :] = jnp.where(is_att, kt, 0.0).astype(BF16)
        kki_ref[ts, :] = jnp.where(is_att, 0.0, kt).astype(BF16)

    c0 = 0
    not_first = i % tiles_per_seq != 0

    @pl.when(i == 0)
    def _():
        halo_ref[...] = jnp.zeros(halo_ref.shape, F32)

    for g in range(2 * ML_W // LANES // 2):
        cs = slice(c0 + g * 256, c0 + (g + 1) * 256)
        ls = slice(g * 256, (g + 1) * 256)
        ext_ref[0:8, :] = jnp.where(not_first, halo_ref[g], 0.0)
        ext_ref[8:, :] = _bdot(xb, wn_ref[:, cs])
        halo_ref[g] = ext_ref[TM:, :]
        for r0 in range(0, TM, CONV_SLAB):
            y = cb_ref[:, ls]
            for k in range(CONV_WIDTH):
                y = y + cw_ref[CONV_WIDTH - 1 - k: CONV_WIDTH - k, ls] * ext_ref[8 - k + r0: 8 - k + r0 + CONV_SLAB, :]
            y = y * jax.nn.sigmoid(y)
            rows = slice(r0, r0 + CONV_SLAB)
            if g < 2:
                mq_ref[rows, g * 256:(g + 1) * 256] = y.astype(mq_ref.dtype)
            else:
                mk_ref[rows, (g - 2) * 256:(g - 1) * 256] = (y * (ML_HEAD_DIM ** -0.5)).astype(mk_ref.dtype)
    c1 = c0 + 2 * ML_W
    mv_ref[...] = _bdot(xb, wn_ref[:, c1: c1 + ML_W]).astype(mv_ref.dtype)
    og_ref[...] = jax.nn.sigmoid(_bdot(xb, wn_ref[:, c1 + ML_W: c1 + 2 * ML_W])).astype(og_ref.dtype)

    r0 = (2 * ATT_HEADS + 2) * HD
    vt_ref[0] = t[r0: r0 + HD].astype(vt_ref.dtype)
    g = t[r0 + HD:] + gb_ref[...]
    row = lax.broadcasted_iota(I32, (16, 1), 0)
    gt_ref[0] = jnp.where(row < IDX_HEADS, g * IDX_W_SCALE,
                          jnp.where(row < IDX_HEADS + ML_HEADS, g,
                                    jnp.minimum(g, 0.0) - jnp.log1p(jnp.exp(-jnp.abs(g)))))


def _proj_call(x2, pos3, invf, wt, wn, cw, cb, gb, B, S):
    N = x2.shape[0]
    tps = S // TM
    W = ATT_HEADS * TQ
    row = lambda w: pl.BlockSpec((TM, w), lambda i: (i, 0))
    out_shape = [jax.ShapeDtypeStruct((N // TQ, LANES, W), BF16), jax.ShapeDtypeStruct((N, LANES), BF16),
                 jax.ShapeDtypeStruct((N, LANES), BF16)] + [jax.ShapeDtypeStruct((N, ML_W), BF16)] * 4 + [
        jax.ShapeDtypeStruct((B, ATT_HEAD_DIM, S), BF16), jax.ShapeDtypeStruct((B, 16, S), F32)]
    out_specs = [pl.BlockSpec((TM // TQ, LANES, W), lambda i: (i, 0, 0)), row(LANES), row(LANES)] + [row(ML_W)] * 4 + [
        pl.BlockSpec((1, ATT_HEAD_DIM, TM), lambda i: (i // tps, 0, i % tps)),
        pl.BlockSpec((1, 16, TM), lambda i: (i // tps, 0, i % tps))]
    return pl.pallas_call(
        functools.partial(_proj_kernel, tps),
        grid=(N // TM,),
        in_specs=[row(D_MODEL), pl.BlockSpec((1, 1, TM), lambda i: (i, 0, 0)), _const_spec(invf.shape),
                  _const_spec(wt.shape), _const_spec(wn.shape),
                  _const_spec(cw.shape), _const_spec(cb.shape), _const_spec(gb.shape)],
        out_specs=out_specs, out_shape=out_shape,
        scratch_shapes=[pltpu.VMEM((TM + 8, 256), F32), pltpu.VMEM((2 * ML_W // 256, 8, 256), F32)],
        compiler_params=pltpu.CompilerParams(dimension_semantics=("arbitrary",), vmem_limit_bytes=VMEM_LIMIT),
        name="proj",
    )(x2, pos3, invf, wt, wn, cw, cb, gb)


def _attn_kernel(k_sel, *refs):
    for slot in range(Q_BLOCKS):
        _attn_block(k_sel, slot, *refs)


def _attn_block(k_sel, slot, qs_ref, kka_ref, kki_ref, vt_ref, gq_ref, o_ref, sc_ref, scb_ref, s_ref):
    j = pl.program_id(1) * Q_BLOCKS + slot
    nkeys = (j + 1) * TQ
    qpos = j * TQ + lax.broadcasted_iota(I32, (1, TQ), 1)
    wq = gq_ref[0, 0:IDX_HEADS, slot * TQ:(slot + 1) * TQ]
    W = ATT_HEADS * TQ

    def for_chunks(body, carry):
        carry = lax.fori_loop(0, nkeys // KC, lambda c, cr: body(pl.multiple_of(c * KC, KC), KC, cr), carry)
        k0, size = (nkeys // KC) * KC, KC // 2
        while size >= TQ:
            present = (nkeys & size) != 0
            carry = lax.cond(present, lambda cr, k0=k0, size=size: body(pl.multiple_of(k0, size), size, cr),
                             lambda cr: cr, carry)
            k0, size = k0 + jnp.where(present, size, 0), size // 2
        return carry

    def fold(a, op):
        out = None
        for r0 in range(0, a.shape[0], 64):
            parts = [a[r0 + r * 8: r0 + (r + 1) * 8] for r in range(8)]
            while len(parts) > 1:
                parts = [op(parts[i], parts[i + 1]) for i in range(0, len(parts), 2)]
            out = parts[0] if out is None else op(out, parts[0])
        return out

    def phase_a(k0, size, carry):
        smin, smax = carry
        kc = kki_ref[0, pl.ds(k0, size), :]
        sc = jnp.zeros((size, TQ), F32)
        for h in range(IDX_HEADS):
            sc = sc + jnp.maximum(_bdot(kc, qs_ref[slot, :, h * TQ:(h + 1) * TQ]), 0.0) * wq[h:h + 1, :]
        causal = lax.broadcasted_iota(I32, (size, TQ), 0) + k0 <= qpos
        sc_ref[pl.ds(k0, size), :] = jnp.where(causal, sc, -jnp.inf)
        scb_ref[pl.ds(k0, size), :] = jnp.where(causal, sc, -jnp.inf).astype(BF16)
        smin = jnp.minimum(smin, fold(jnp.where(causal, sc, jnp.inf), jnp.minimum))
        smax = jnp.maximum(smax, fold(jnp.where(causal, sc, -jnp.inf), jnp.maximum))
        return smin, smax

    smin, smax = for_chunks(phase_a, (jnp.full((8, TQ), jnp.inf, F32), jnp.full((8, TQ), -jnp.inf, F32)))

    SUB = 64

    def count(pred):
        def body(k0, size, acc):
            for r0 in range(0, size, SUB):
                acc = acc + fold(pred(sc_ref[pl.ds(k0 + r0, SUB), :]).astype(I32), jnp.add)
            return acc
        return jnp.sum(for_chunks(body, jnp.zeros((8, TQ), I32)), axis=0, keepdims=True)

    def max_below(bound):
        def body(k0, size, acc):
            for r0 in range(0, size, SUB):
                s = sc_ref[pl.ds(k0 + r0, SUB), :]
                acc = jnp.maximum(acc, fold(jnp.where(s < bound, s, -jnp.inf), jnp.maximum))
            return acc
        return jnp.max(for_chunks(body, jnp.full((8, TQ), -jnp.inf, F32)), axis=0, keepdims=True)

    kq = jnp.minimum(qpos + 1, k_sel)

    def bisect(_, st):
        lo, hi, hi_ok, clo = st
        x = 0.5 * lo + 0.5 * hi
        c = count(lambda s: s >= x)
        ge = c >= kq
        return jnp.where(ge, x, lo), jnp.where(ge, hi, x), jnp.where(ge, hi_ok, 1), jnp.where(ge, c, clo)

    def count16(xb):
        one, zero = jnp.ones((16, TQ), BF16), jnp.zeros((16, TQ), BF16)

        def body(k0, size, acc):
            part = [zero, zero]
            for r0 in range(0, size, 4 * SUB):
                slab = scb_ref[pl.ds(k0 + r0, 4 * SUB), :]
                for n in range(4 * SUB // 16):
                    part[n % 2] = part[n % 2] + jnp.where(slab[n * 16:(n + 1) * 16] >= xb, one, zero)
            return acc + (part[0] + part[1]).astype(F32)
        return jnp.sum(for_chunks(body, jnp.zeros((16, TQ), F32)), axis=0, keepdims=True)

    def bisect16(_, st):
        lo, hi, hi_ok, clo = st
        xb = (0.5 * lo + 0.5 * hi).astype(BF16)
        x = xb.astype(F32)
        ge = count16(xb) >= kq.astype(F32)
        lo_new = jnp.where(ge, jnp.maximum(lo, x - jnp.maximum(jnp.abs(x) * 2.0 ** -7, 1e-30)), lo)
        return (lo_new, jnp.where(ge, hi, jnp.minimum(hi, x)), jnp.where(ge, hi_ok, 1),
                jnp.where(lo_new > lo, -1, clo))

    st = (jnp.min(smin, axis=0, keepdims=True), jnp.max(smax, axis=0, keepdims=True),
          jnp.zeros((1, TQ), I32), qpos + 1)
    st = lax.fori_loop(0, N_BISECT16, bisect16, st)
    st = lax.fori_loop(0, N_BISECT, bisect, st)

    def exact_check(st, tie):
        lo, hi, hi_ok, clo = st
        rem = (clo != kq) & (tie == 0)
        v = max_below(jnp.where(hi_ok > 0, hi, jnp.inf))
        cv = count(lambda s: s >= v)
        hit = rem & (cv >= kq)
        miss = rem & (cv < kq)
        st = (jnp.where(hit, v, lo), jnp.where(miss, v, hi), jnp.where(miss, 1, hi_ok), jnp.where(hit, kq, clo))
        return st, jnp.where(hit & (cv > kq), 1, tie)

    def flags(st, tie):
        return jnp.max(jnp.where((st[3] != kq) & (tie == 0), 2, 0) + tie)

    st, tie = exact_check(st, jnp.zeros((1, TQ), I32))

    def refine(carry):
        st, tie, _, trips = carry
        st, tie = exact_check(lax.fori_loop(0, 2, bisect, st), tie)
        return st, tie, flags(st, tie), trips + 1

    st, tie, code, _ = lax.while_loop(lambda c: (c[2] >= 2) & (c[3] < MAX_REFINE), refine,
                                      (st, tie, flags(st, tie), 0))
    thr = st[0]

    def tie_fix(_):
        need = (kq - count(lambda s: s > thr)).astype(F32)

        def body(k0, size, base):
            tri = lax.broadcasted_iota(I32, (size, size), 0) >= lax.broadcasted_iota(I32, (size, size), 1)
            s = sc_ref[pl.ds(k0, size), :]
            eq = (s == thr) & (tie > 0)
            rank = _bdot(tri.astype(BF16), eq.astype(BF16)) + base
            sc_ref[pl.ds(k0, size), :] = jnp.where(eq & (rank > need), -jnp.inf, s)
            return rank[size - 1:size, :]

        for_chunks(body, jnp.zeros((1, TQ), F32))
        return 0

    lax.cond(code % 2 == 1, tie_fix, lambda _: 0, 0)

    def logits(k0, size, mx):
        kc = kka_ref[0, pl.ds(k0, size), :]
        bias = jnp.where(sc_ref[pl.ds(k0, size), :] >= thr, 0.0, -jnp.inf)
        cols = []
        for h in range(ATT_HEADS):
            sh = _bdot(kc, qs_ref[slot, :, h * TQ:(h + 1) * TQ]) + bias
            s_ref[pl.ds(k0, size), h * TQ:(h + 1) * TQ] = sh
            cols.append(fold(sh, jnp.maximum))
        return jnp.maximum(mx, jnp.concatenate(cols, axis=1))

    mx = for_chunks(logits, jnp.full((8, W), -jnp.inf, F32))
    mx = jnp.max(mx, axis=0, keepdims=True)

    def weighted(k0, size, carry):
        l, acc = carry
        vc = vt_ref[0, :, pl.ds(k0, size)]
        cols, pv = [], []
        for h in range(ATT_HEADS):
            hs = slice(h * TQ, (h + 1) * TQ)
            p = jnp.exp2(s_ref[pl.ds(k0, size), hs] - mx[:, hs])
            cols.append(fold(p, jnp.add))
            pv.append(_bdot(vc, p.astype(BF16)))
        return l + jnp.concatenate(cols, axis=1), acc + jnp.concatenate(pv, axis=1)

    l, acc = for_chunks(weighted, (jnp.zeros((8, W), F32), jnp.zeros((ATT_HEAD_DIM, W), F32)))
    out_t = acc / jnp.sum(l, axis=0, keepdims=True)
    out_t = jnp.concatenate([out_t[:, h * TQ:(h + 1) * TQ] for h in range(ATT_HEADS)], axis=0)
    o_ref[0, slot * TQ:(slot + 1) * TQ, :] = out_t.T.astype(o_ref.dtype)


def _attn_call(qs, kka, kki, vt, gt, B, S):
    k_sel = min(TOPK_MAX, S // 4)
    W = ATT_HEADS * TQ
    nstep = S // (Q_BLOCKS * TQ)
    kspec = pl.BlockSpec((1, S, LANES), lambda b, j: (b, 0, 0))
    return pl.pallas_call(
        functools.partial(_attn_kernel, k_sel),
        grid=(B, nstep),
        in_specs=[pl.BlockSpec((Q_BLOCKS, LANES, W), lambda b, j: (b * nstep + j, 0, 0)), kspec, kspec,
                  pl.BlockSpec((1, ATT_HEAD_DIM, S), lambda b, j: (b, 0, 0)),
                  pl.BlockSpec((1, 16, Q_BLOCKS * TQ), lambda b, j: (b, 0, j))],
        out_specs=pl.BlockSpec((1, Q_BLOCKS * TQ, ATT_W), lambda b, j: (b, j, 0)),
        out_shape=jax.ShapeDtypeStruct((B, S, ATT_W), BF16),
        scratch_shapes=[pltpu.VMEM((S, TQ), F32), pltpu.VMEM((S, TQ), BF16), pltpu.VMEM((S, W), F32)],
        compiler_params=pltpu.CompilerParams(dimension_semantics=("arbitrary", "arbitrary"),
                                             vmem_limit_bytes=VMEM_LIMIT),
        name="attn",
    )(qs, kka, kki, vt, gt)


def _segment_scan(x, op, identity, seg):
    lane = lax.broadcasted_iota(I32, x.shape, 1) % seg
    s = 1
    while s < seg:
        x = op(x, jnp.where(lane >= s, pltpu.roll(x, s, 1), identity))
        s *= 2
    return x


def _split3(x):
    hi = x.astype(BF16)
    r = x - hi.astype(F32)
    mid = r.astype(BF16)
    return hi, mid, (r - mid.astype(F32)).astype(BF16)


def _mlstm_kernel(nchunks, mq_ref, mk_ref, mv_ref, og_ref, gt_ref, ng_ref, y_ref, c_ref, g_ref):
    L, H, Dh = ML_L, ML_HEADS, ML_HEAD_DIM
    c_ref[...] = jnp.zeros_like(c_ref)
    r_i = lax.broadcasted_iota(I32, (L, L), 0)
    c_i = lax.broadcasted_iota(I32, (L, L), 1)
    eye = (r_i == c_i).astype(BF16)
    eye3 = jnp.concatenate([eye, eye, eye], axis=1)
    tril = r_i >= c_i
    ones_b = jnp.ones((L, LANES), BF16)
    ones2 = jnp.ones((2 * Dh, LANES), BF16)
    heads = range(H)

    gi = gt_ref[0, IDX_HEADS:IDX_HEADS + H, :]
    lf = gt_ref[0, IDX_HEADS + H:IDX_HEADS + 2 * H, :]
    b_seq = _segment_scan(lf, jnp.add, 0.0, L)
    a_seq = gi - b_seq
    g_ref[...] = jnp.concatenate([b_seq, a_seq, _segment_scan(a_seq, jnp.maximum, -jnp.inf, L),
                                  jnp.zeros_like(b_seq)], axis=0)

    def lane_sum(x):
        hi = x.astype(BF16)
        return _bdot(jnp.concatenate([hi, (x - hi.astype(F32)).astype(BF16)], axis=1), ones2)

    def chunk(c, m_prev):
        t0 = pl.multiple_of(c * L, L)
        rows = pl.ds(t0, L)
        stacked = g_ref[:, rows]
        a_all = stacked[H:2 * H]
        m_last = jnp.maximum(stacked[2 * H:3 * H, L - 1:L], m_prev)
        m_next = stacked[0:H, L - 1:L] + m_last
        cols = _dot_nt(eye3, jnp.concatenate(_split3(stacked), axis=1))
        b_col = [cols[:, h:h + 1] for h in heads]
        a_col = [cols[:, H + h:H + h + 1] for h in heads]
        mcol = [jnp.maximum(cols[:, 2 * H + h:2 * H + h + 1], m_prev[h:h + 1]) for h in heads]
        hsl = [slice(h * Dh, (h + 1) * Dh) for h in heads]
        q = [mq_ref[0, rows, hsl[h]] for h in heads]
        k = [mk_ref[0, rows, hsl[h]] for h in heads]
        v_aug = [jnp.concatenate([mv_ref[0, rows, hsl[h]], ones_b], axis=1) for h in heads]
        cst = [c_ref[h] for h in heads]
        qk = [_dot_nt(q[h], k[h]) for h in heads]
        qc = [_bdot(q[h], cst[h].astype(BF16)) for h in heads]
        w_intra = [jnp.where(tril, jnp.exp(a_all[h:h + 1] - mcol[h]), 0.0) for h in heads]
        sv = [_bdot((qk[h] * w_intra[h]).astype(BF16), v_aug[h]) for h in heads]
        for h in heads:
            wv = (v_aug[h].astype(F32) * jnp.exp(a_col[h] - m_last[h:h + 1])).astype(BF16)
            c_ref[h] = jnp.exp(m_prev[h:h + 1] - m_last[h:h + 1]) * cst[h] + _dot_tn(k[h], wv)
        haug = [jnp.exp(m_prev[h:h + 1] - mcol[h]) * qc[h] + sv[h] for h in heads]
        hm = [haug[h][:, :Dh] / jnp.maximum(jnp.abs(haug[h][:, Dh:]), jnp.exp(-(b_col[h] + mcol[h]))) for h in heads]
        d = [hm[h] - lane_sum(hm[h]) * (1.0 / Dh) for h in heads]
        var = [lane_sum(d[h] * d[h]) * (1.0 / Dh) for h in heads]
        for h in heads:
            yn = d[h] * lax.rsqrt(var[h] + LN_EPS) * ng_ref[:, hsl[h]]
            y_ref[0, rows, hsl[h]] = (og_ref[0, rows, hsl[h]].astype(F32) * yn).astype(y_ref.dtype)
        return m_next

    lax.fori_loop(0, nchunks, chunk, jnp.zeros((H, 1), F32), unroll=True)


def _mlstm_call(mq, mk, mv, og, gt, ng, B, S):
    seq = pl.BlockSpec((1, S, ML_W), lambda b: (b, 0, 0))
    return pl.pallas_call(
        functools.partial(_mlstm_kernel, S // ML_L),
        grid=(B,),
        in_specs=[seq, seq, seq, seq, pl.BlockSpec((1, 16, S), lambda b: (b, 0, 0)), _const_spec(ng.shape)],
        out_specs=seq,
        out_shape=jax.ShapeDtypeStruct((B, S, ML_W), BF16),
        scratch_shapes=[pltpu.VMEM((ML_HEADS, ML_HEAD_DIM, 2 * ML_HEAD_DIM), F32),
                        pltpu.VMEM((4 * ML_HEADS, S), F32)],
        compiler_params=pltpu.CompilerParams(dimension_semantics=("arbitrary",), vmem_limit_bytes=VMEM_LIMIT),
        name="mlstm",
    )(mq, mk, mv, og, gt, ng)


def _merge_kernel(x_ref, ya_ref, yb_ref, wg_ref, wua_ref, wub_ref, wo_ref, g_ref, b_ref, h_ref):
    x = x_ref[...]
    xb = x.astype(BF16)
    ma = jax.nn.sigmoid(_bdot(xb, wg_ref[:, :D_MODEL])) * _bdot(ya_ref[...], wua_ref[...])
    mb = jax.nn.sigmoid(_bdot(xb, wg_ref[:, D_MODEL:])) * _bdot(yb_ref[...], wub_ref[...])
    r = DEEPNORM_ALPHA * x + _bdot((ma + mb).astype(BF16), wo_ref[...])
    h_ref[...] = _layer_norm(r, g_ref[...], b_ref[...])


def _merge_call(x2, ya, yb, wg, wua, wub, wo, g, b):
    N = x2.shape[0]
    row = lambda w: pl.BlockSpec((TM_FFN, w), lambda i: (i, 0))
    return pl.pallas_call(
        _merge_kernel,
        grid=(N // TM_FFN,),
        in_specs=[row(D_MODEL), row(ATT_W), row(ML_W)] + [_const_spec(a.shape) for a in (wg, wua, wub, wo, g, b)],
        out_specs=row(D_MODEL),
        out_shape=jax.ShapeDtypeStruct((N, D_MODEL), F32),
        compiler_params=pltpu.CompilerParams(dimension_semantics=("arbitrary",), vmem_limit_bytes=VMEM_LIMIT),
        name="merge",
    )(x2, ya, yb, wg, wua, wub, wo, g, b)


def _ffn_kernel(h_ref, p_ref, w1_ref, w2_ref, wpg_ref, wpp_ref, g_ref, b_ref, o_ref):
    h = h_ref[...]
    hb = h.astype(BF16)
    ff = jnp.zeros_like(h)
    for c in range(D_FF // D_MODEL):
        cs = slice(c * D_MODEL, (c + 1) * D_MODEL)
        a = jnp.maximum(_bdot(hb, w1_ref[:, cs]), 0.0)
        ff = ff + _bdot((a * a).astype(BF16), w2_ref[cs, :])
    r = DEEPNORM_ALPHA * h + ff
    r = r + jax.nn.sigmoid(_bdot(r.astype(BF16), wpg_ref[...])) * _bdot(p_ref[...].astype(BF16), wpp_ref[...])
    o_ref[...] = _layer_norm(r, g_ref[...], b_ref[...])


def _ffn_call(h1, p2, w1, w2, wpg, wpp, g, b):
    N = h1.shape[0]
    row = lambda w: pl.BlockSpec((TM_FFN, w), lambda i: (i, 0))
    return pl.pallas_call(
        _ffn_kernel,
        grid=(N // TM_FFN,),
        in_specs=[row(D_MODEL), row(PLE_DIM)] + [_const_spec(a.shape) for a in (w1, w2, wpg, wpp, g, b)],
        out_specs=row(D_MODEL),
        out_shape=jax.ShapeDtypeStruct((N, D_MODEL), F32),
        compiler_params=pltpu.CompilerParams(dimension_semantics=("arbitrary",), vmem_limit_bytes=VMEM_LIMIT),
        name="ffn",
    )(h1, p2, w1, w2, wpg, wpp, g, b)


def _split_w_in_t(wt):
    parts, off = {}, 0
    for name, width in _SPLIT:
        parts[name] = wt[off:off + width]
        off += width
    return parts


def kernel(x, p, positions, w_in, conv_w, conv_b, b_igate, b_fgate, ml_norm_g, w_up_a, w_up_b, w_out,
           ln1_g, ln1_b, w_ff1, w_ff2, w_ple_gate, w_ple_proj, ln2_g, ln2_b):
    B, S, D = x.shape
    assert D == D_MODEL and S % TM == 0 and S % KC == 0 and w_in.shape[0] == DEPTH == 1
    N = B * S
    x2 = x.reshape(N, D)
    pos3 = positions.reshape(N // TM, 1, TM)
    inv_freq = 1.0 / (ROPE_THETA ** (jnp.arange(0, ATT_HEAD_DIM, 2, dtype=F32) / ATT_HEAD_DIM))
    invf = inv_freq.reshape(ATT_HEAD_DIM // 2, 1)

    w = _split_w_in_t(w_in[0].T)
    wt = jnp.concatenate([w['att_q'] * (ATT_HEAD_DIM ** -0.5 * LOG2E), w['idx_q'], w['att_k'], w['idx_k'],
                          w['att_v'], w['idx_w'], w['ml_i'], w['ml_f']], axis=0).astype(BF16)
    wn = jnp.concatenate([w['ml_q'], w['ml_k'], w['ml_v'], w['ml_o']], axis=0).astype(BF16).T
    gb = jnp.concatenate([jnp.zeros((IDX_HEADS,), F32), b_igate[0], b_fgate[0]]).reshape(16, 1)
    wg = jnp.concatenate([w['gate_a'], w['gate_b']], axis=0).astype(BF16).T

    qs, kka, kki, mq, mk, mv, og, vt, gt = _proj_call(x2, pos3, invf, wt, wn, conv_w[0], conv_b[0].reshape(1, -1),
                                                      gb, B, S)
    r3 = lambda a: a.reshape(B, S, a.shape[-1])
    ya = _attn_call(qs, r3(kka), r3(kki), vt, gt, B, S)
    yb = _mlstm_call(r3(mq), r3(mk), r3(mv), r3(og), gt, ml_norm_g[0].reshape(1, -1), B, S)
    h1 = _merge_call(x2, ya.reshape(N, ATT_W), yb.reshape(N, ML_W), wg, w_up_a[0].astype(BF16),
                     w_up_b[0].astype(BF16), w_out[0].astype(BF16), ln1_g[0].reshape(1, -1), ln1_b[0].reshape(1, -1))
    out = _ffn_call(h1, p[0].reshape(N, PLE_DIM), w_ff1[0].astype(BF16), w_ff2[0].astype(BF16),
                    w_ple_gate[0].astype(BF16), w_ple_proj[0].astype(BF16),
                    ln2_g[0].reshape(1, -1), ln2_b[0].reshape(1, -1))
    return out.reshape(B, S, D)
```

```python
import functools

import jax
import jax.numpy as jnp
from jax import lax
from jax.experimental import pallas as pl
from jax.experimental.pallas import tpu as pltpu

F32 = jnp.float32
BF16 = jnp.bfloat16
I32 = jnp.int32

D_MODEL = 1024
ATT_HEADS = 8
ATT_HEAD_DIM = 64
IDX_HEADS = 8
IDX_DIM = 64
TOPK_MAX = 256
ML_HEADS = 4
ML_HEAD_DIM = 128
CONV_WIDTH = 4
D_FF = 4 * D_MODEL
PLE_DIM = 256
ROPE_THETA = 10000.0
LN_EPS = 1e-5
DEPTH = 1
DEEPNORM_ALPHA = (2.0 * DEPTH) ** 0.25
IDX_W_SCALE = (IDX_HEADS ** -0.5) * (IDX_DIM ** -0.5)
LOG2E = 1.4426950408889634

ATT_W = ATT_HEADS * ATT_HEAD_DIM
IDX_QW = IDX_HEADS * IDX_DIM
ML_W = ML_HEADS * ML_HEAD_DIM
assert ATT_HEAD_DIM == IDX_DIM == 64 and ATT_HEADS == IDX_HEADS == 8

_SPLIT = (
    ('att_q', ATT_W), ('att_k', ATT_HEAD_DIM), ('att_v', ATT_HEAD_DIM),
    ('idx_q', IDX_QW), ('idx_k', IDX_DIM), ('idx_w', IDX_HEADS),
    ('ml_q', ML_W), ('ml_k', ML_W), ('ml_v', ML_W),
    ('ml_i', ML_HEADS), ('ml_f', ML_HEADS), ('ml_o', ML_W),
    ('gate_a', D_MODEL), ('gate_b', D_MODEL),
)

LANES = 128
V7X_VMEM_BYTES = 64 * 1024 * 1024
VMEM_LIMIT = V7X_VMEM_BYTES - 8 * 1024 * 1024

TM = 1024
TM_FFN = 1024
CONV_SLAB = 512
TQ = 256
Q_BLOCKS = 2
KC = 1024
MAX_REFINE = 160
assert KC == 4 * TQ and KC & (KC - 1) == 0
N_BISECT16 = 10
N_BISECT = 7
ML_L = 256


def _const_spec(shape):
    nd = len(shape)
    return pl.BlockSpec(shape, lambda *_: (0,) * nd, pipeline_mode=pl.Buffered(1))


def _bdot(a, b):
    return jnp.dot(a, b, preferred_element_type=F32)


def _dot_nt(a, b):
    return lax.dot_general(a, b, (((1,), (1,)), ((), ())), preferred_element_type=F32)


def _dot_tn(a, b):
    return lax.dot_general(a, b, (((0,), (0,)), ((), ())), preferred_element_type=F32)


def _layer_norm(r, g, b):
    mu = jnp.mean(r, axis=-1, keepdims=True)
    d = r - mu
    var = jnp.mean(d * d, axis=-1, keepdims=True)
    return d * lax.rsqrt(var + LN_EPS) * g + b


def _proj_kernel(tiles_per_seq, x_ref, pos_ref, invf_ref, wt_ref, wn_ref, cw_ref, cb_ref, gb_ref,
                 qs_ref, kka_ref, kki_ref, mq_ref, mk_ref, mv_ref, og_ref, vt_ref, gt_ref, ext_ref, halo_ref):
    i = pl.program_id(0)
    xb = x_ref[...].astype(BF16)
    HD, HALF = ATT_HEAD_DIM, ATT_HEAD_DIM // 2
    t = _dot_nt(wt_ref[...], xb)
    ang = invf_ref[...] * pos_ref[0].astype(F32)
    cos, sin = jnp.cos(ang), jnp.sin(ang)

    def rope(blk, ts):
        x1, x2 = t[blk * HD: blk * HD + HALF, ts], t[blk * HD + HALF: (blk + 1) * HD, ts]
        return x1 * cos[:, ts] - x2 * sin[:, ts], x2 * cos[:, ts] + x1 * sin[:, ts]

    is_att = lax.broadcasted_iota(I32, (1, LANES), 1) < HD
    for g in range(TM // TQ):
        ts = slice(g * TQ, (g + 1) * TQ)
        for h in range(ATT_HEADS):
            for r, piece in enumerate(rope(h, ts) + rope(ATT_HEADS + h, ts)):
                qs_ref[g, r * HALF:(r + 1) * HALF, h * TQ:(h + 1) * TQ] = piece.astype(BF16)
        kt = jnp.concatenate(rope(2 * ATT_HEADS, ts) + rope(2 * ATT_HEADS + 1, ts), axis=0).T
        kka_ref[ts, :] = jnp.where(is_att, kt, 0.0).astype(BF16)
        kki_ref[ts, :] = jnp.where(is_att, 0.0, kt).astype(BF16)

    c0 = 0
    not_first = i % tiles_per_seq != 0

    @pl.when(i == 0)
    def _():
        halo_ref[...] = jnp.zeros(halo_ref.shape, F32)

    for g in range(2 * ML_W // LANES // 2):
        cs = slice(c0 + g * 256, c0 + (g + 1) * 256)
        ls = slice(g * 256, (g + 1) * 256)
        ext_ref[0:8, :] = jnp.where(not_first, halo_ref[g], 0.0)
        ext_ref[8:, :] = _bdot(xb, wn_ref[:, cs])
        halo_ref[g] = ext_ref[TM:, :]
        for r0 in range(0, TM, CONV_SLAB):
            y = cb_ref[:, ls]
            for k in range(CONV_WIDTH):
                y = y + cw_ref[CONV_WIDTH - 1 - k: CONV_WIDTH - k, ls] * ext_ref[8 - k + r0: 8 - k + r0 + CONV_SLAB, :]
            y = y * jax.nn.sigmoid(y)
            rows = slice(r0, r0 + CONV_SLAB)
            if g < 2:
                mq_ref[rows, g * 256:(g + 1) * 256] = y.astype(mq_ref.dtype)
            else:
                mk_ref[rows, (g - 2) * 256:(g - 1) * 256] = (y * (ML_HEAD_DIM ** -0.5)).astype(mk_ref.dtype)
    c1 = c0 + 2 * ML_W
    mv_ref[...] = _bdot(xb, wn_ref[:, c1: c1 + ML_W]).astype(mv_ref.dtype)
    og_ref[...] = jax.nn.sigmoid(_bdot(xb, wn_ref[:, c1 + ML_W: c1 + 2 * ML_W])).astype(og_ref.dtype)

    r0 = (2 * ATT_HEADS + 2) * HD
    vt_ref[0] = t[r0: r0 + HD].astype(vt_ref.dtype)
    g = t[r0 + HD:] + gb_ref[...]
    row = lax.broadcasted_iota(I32, (16, 1), 0)
    gt_ref[0] = jnp.where(row < IDX_HEADS, g * IDX_W_SCALE,
                          jnp.where(row < IDX_HEADS + ML_HEADS, g,
                                    jnp.minimum(g, 0.0) - jnp.log1p(jnp.exp(-jnp.abs(g)))))


def _proj_call(x2, pos3, invf, wt, wn, cw, cb, gb, B, S):
    N = x2.shape[0]
    tps = S // TM
    W = ATT_HEADS * TQ
    row = lambda w: pl.BlockSpec((TM, w), lambda i: (i, 0))
    out_shape = [jax.ShapeDtypeStruct((N // TQ, LANES, W), BF16), jax.ShapeDtypeStruct((N, LANES), BF16),
                 jax.ShapeDtypeStruct((N, LANES), BF16)] + [jax.ShapeDtypeStruct((N, ML_W), BF16)] * 4 + [
        jax.ShapeDtypeStruct((B, ATT_HEAD_DIM, S), BF16), jax.ShapeDtypeStruct((B, 16, S), F32)]
    out_specs = [pl.BlockSpec((TM // TQ, LANES, W), lambda i: (i, 0, 0)), row(LANES), row(LANES)] + [row(ML_W)] * 4 + [
        pl.BlockSpec((1, ATT_HEAD_DIM, TM), lambda i: (i // tps, 0, i % tps)),
        pl.BlockSpec((1, 16, TM), lambda i: (i // tps, 0, i % tps))]
    return pl.pallas_call(
        functools.partial(_proj_kernel, tps),
        grid=(N // TM,),
        in_specs=[row(D_MODEL), pl.BlockSpec((1, 1, TM), lambda i: (i, 0, 0)), _const_spec(invf.shape),
                  _const_spec(wt.shape), _const_spec(wn.shape),
                  _const_spec(cw.shape), _const_spec(cb.shape), _const_spec(gb.shape)],
        out_specs=out_specs, out_shape=out_shape,
        scratch_shapes=[pltpu.VMEM((TM + 8, 256), F32), pltpu.VMEM((2 * ML_W // 256, 8, 256), F32)],
        compiler_params=pltpu.CompilerParams(dimension_semantics=("arbitrary",), vmem_limit_bytes=VMEM_LIMIT),
        name="proj",
    )(x2, pos3, invf, wt, wn, cw, cb, gb)


def _attn_kernel(k_sel, *refs):
    for slot in range(Q_BLOCKS):
        _attn_block(k_sel, slot, *refs)


def _attn_block(k_sel, slot, qs_ref, kka_ref, kki_ref, vt_ref, gq_ref, o_ref, sc_ref, scb_ref, s_ref):
    j = pl.program_id(1) * Q_BLOCKS + slot
    nkeys = (j + 1) * TQ
    qpos = j * TQ + lax.broadcasted_iota(I32, (1, TQ), 1)
    wq = gq_ref[0, 0:IDX_HEADS, slot * TQ:(slot + 1) * TQ]
    W = ATT_HEADS * TQ

    def for_chunks(body, carry):
        carry = lax.fori_loop(0, nkeys // KC, lambda c, cr: body(pl.multiple_of(c * KC, KC), KC, cr), carry)
        k0, size = (nkeys // KC) * KC, KC // 2
        while size >= TQ:
            present = (nkeys & size) != 0
            carry = lax.cond(present, lambda cr, k0=k0, size=size: body(pl.multiple_of(k0, size), size, cr),
                             lambda cr: cr, carry)
            k0, size = k0 + jnp.where(present, size, 0), size // 2
        return carry

    def fold(a, op):
        out = None
        for r0 in range(0, a.shape[0], 64):
            parts = [a[r0 + r * 8: r0 + (r + 1) * 8] for r in range(8)]
            while len(parts) > 1:
                parts = [op(parts[i], parts[i + 1]) for i in range(0, len(parts), 2)]
            out = parts[0] if out is None else op(out, parts[0])
        return out

    def phase_a(k0, size, carry):
        smin, smax = carry
        kc = kki_ref[0, pl.ds(k0, size), :]
        sc = jnp.zeros((size, TQ), F32)
        for h in range(IDX_HEADS):
            sc = sc + jnp.maximum(_bdot(kc, qs_ref[slot, :, h * TQ:(h + 1) * TQ]), 0.0) * wq[h:h + 1, :]
        causal = lax.broadcasted_iota(I32, (size, TQ), 0) + k0 <= qpos
        sc_ref[pl.ds(k0, size), :] = jnp.where(causal, sc, -jnp.inf)
        scb_ref[pl.ds(k0, size), :] = jnp.where(causal, sc, -jnp.inf).astype(BF16)
        smin = jnp.minimum(smin, fold(jnp.where(causal, sc, jnp.inf), jnp.minimum))
        smax = jnp.maximum(smax, fold(jnp.where(causal, sc, -jnp.inf), jnp.maximum))
        return smin, smax

    smin, smax = for_chunks(phase_a, (jnp.full((8, TQ), jnp.inf, F32), jnp.full((8, TQ), -jnp.inf, F32)))

    SUB = 64

    def count(pred):
        def body(k0, size, acc):
            for r0 in range(0, size, SUB):
                acc = acc + fold(pred(sc_ref[pl.ds(k0 + r0, SUB), :]).astype(I32), jnp.add)
            return acc
        return jnp.sum(for_chunks(body, jnp.zeros((8, TQ), I32)), axis=0, keepdims=True)

    def max_below(bound):
        def body(k0, size, acc):
            for r0 in range(0, size, SUB):
                s = sc_ref[pl.ds(k0 + r0, SUB), :]
                acc = jnp.maximum(acc, fold(jnp.where(s < bound, s, -jnp.inf), jnp.maximum))
            return acc
        return jnp.max(for_chunks(body, jnp.full((8, TQ), -jnp.inf, F32)), axis=0, keepdims=True)

    kq = jnp.minimum(qpos + 1, k_sel)

    def bisect(_, st):
        lo, hi, hi_ok, clo = st
        x = 0.5 * lo + 0.5 * hi
        c = count(lambda s: s >= x)
        ge = c >= kq
        return jnp.where(ge, x, lo), jnp.where(ge, hi, x), jnp.where(ge, hi_ok, 1), jnp.where(ge, c, clo)

    def count16(xb):
        one, zero = jnp.ones((16, TQ), BF16), jnp.zeros((16, TQ), BF16)

        def body(k0, size, acc):
            part = [zero, zero]
            for r0 in range(0, size, 4 * SUB):
                slab = scb_ref[pl.ds(k0 + r0, 4 * SUB), :]
                for n in range(4 * SUB // 16):
                    part[n % 2] = part[n % 2] + jnp.where(slab[n * 16:(n + 1) * 16] >= xb, one, zero)
            return acc + (part[0] + part[1]).astype(F32)
        return jnp.sum(for_chunks(body, jnp.zeros((16, TQ), F32)), axis=0, keepdims=True)

    def bisect16(_, st):
        lo, hi, hi_ok, clo = st
        xb = (0.5 * lo + 0.5 * hi).astype(BF16)
        x = xb.astype(F32)
        ge = count16(xb) >= kq.astype(F32)
        lo_new = jnp.where(ge, jnp.maximum(lo, x - jnp.maximum(jnp.abs(x) * 2.0 ** -7, 1e-30)), lo)
        return (lo_new, jnp.where(ge, hi, jnp.minimum(hi, x)), jnp.where(ge, hi_ok, 1),
                jnp.where(lo_new > lo, -1, clo))

    st = (jnp.min(smin, axis=0, keepdims=True), jnp.max(smax, axis=0, keepdims=True),
          jnp.zeros((1, TQ), I32), qpos + 1)
    st = lax.fori_loop(0, N_BISECT16, bisect16, st)
    st = lax.fori_loop(0, N_BISECT, bisect, st)

    def exact_check(st, tie):
        lo, hi, hi_ok, clo = st
        rem = (clo != kq) & (tie == 0)
        v = max_below(jnp.where(hi_ok > 0, hi, jnp.inf))
        cv = count(lambda s: s >= v)
        hit = rem & (cv >= kq)
        miss = rem & (cv < kq)
        st = (jnp.where(hit, v, lo), jnp.where(miss, v, hi), jnp.where(miss, 1, hi_ok), jnp.where(hit, kq, clo))
        return st, jnp.where(hit & (cv > kq), 1, tie)

    def flags(st, tie):
        return jnp.max(jnp.where((st[3] != kq) & (tie == 0), 2, 0) + tie)

    st, tie = exact_check(st, jnp.zeros((1, TQ), I32))

    def refine(carry):
        st, tie, _, trips = carry
        st, tie = exact_check(lax.fori_loop(0, 2, bisect, st), tie)
        return st, tie, flags(st, tie), trips + 1

    st, tie, code, _ = lax.while_loop(lambda c: (c[2] >= 2) & (c[3] < MAX_REFINE), refine,
                                      (st, tie, flags(st, tie), 0))
    thr = st[0]

    def tie_fix(_):
        need = (kq - count(lambda s: s > thr)).astype(F32)

        def body(k0, size, base):
            tri = lax.broadcasted_iota(I32, (size, size), 0) >= lax.broadcasted_iota(I32, (size, size), 1)
            s = sc_ref[pl.ds(k0, size), :]
            eq = (s == thr) & (tie > 0)
            rank = _bdot(tri.astype(BF16), eq.astype(BF16)) + base
            sc_ref[pl.ds(k0, size), :] = jnp.where(eq & (rank > need), -jnp.inf, s)
            return rank[size - 1:size, :]

        for_chunks(body, jnp.zeros((1, TQ), F32))
        return 0

    lax.cond(code % 2 == 1, tie_fix, lambda _: 0, 0)

    def logits(k0, size, mx):
        kc = kka_ref[0, pl.ds(k0, size), :]
        bias = jnp.where(sc_ref[pl.ds(k0, size), :] >= thr, 0.0, -jnp.inf)
        cols = []
        for h in range(ATT_HEADS):
            sh = _bdot(kc, qs_ref[slot, :, h * TQ:(h + 1) * TQ]) + bias
            s_ref[pl.ds(k0, size), h * TQ:(h + 1) * TQ] = sh
            cols.append(fold(sh, jnp.maximum))
        return jnp.maximum(mx, jnp.concatenate(cols, axis=1))

    mx = for_chunks(logits, jnp.full((8, W), -jnp.inf, F32))
    mx = jnp.max(mx, axis=0, keepdims=True)

    def weighted(k0, size, acc):
        vc = jnp.concatenate([vt_ref[0, :, pl.ds(k0, size)], jnp.ones((8, size), BF16)], axis=0)
        pv = []
        for h in range(ATT_HEADS):
            hs = slice(h * TQ, (h + 1) * TQ)
            p = jnp.exp2(s_ref[pl.ds(k0, size), hs] - mx[:, hs])
            pv.append(_bdot(vc, p.astype(BF16)))
        return acc + jnp.concatenate(pv, axis=1)

    acc = for_chunks(weighted, jnp.zeros((ATT_HEAD_DIM + 8, W), F32))
    out_t = acc[:ATT_HEAD_DIM] / acc[ATT_HEAD_DIM:ATT_HEAD_DIM + 1]
    out_t = jnp.concatenate([out_t[:, h * TQ:(h + 1) * TQ] for h in range(ATT_HEADS)], axis=0)
    o_ref[0, slot * TQ:(slot + 1) * TQ, :] = out_t.T.astype(o_ref.dtype)


def _attn_call(qs, kka, kki, vt, gt, B, S):
    k_sel = min(TOPK_MAX, S // 4)
    W = ATT_HEADS * TQ
    nstep = S // (Q_BLOCKS * TQ)
    kspec = pl.BlockSpec((1, S, LANES), lambda b, j: (b, 0, 0))
    return pl.pallas_call(
        functools.partial(_attn_kernel, k_sel),
        grid=(B, nstep),
        in_specs=[pl.BlockSpec((Q_BLOCKS, LANES, W), lambda b, j: (b * nstep + j, 0, 0)), kspec, kspec,
                  pl.BlockSpec((1, ATT_HEAD_DIM, S), lambda b, j: (b, 0, 0)),
                  pl.BlockSpec((1, 16, Q_BLOCKS * TQ), lambda b, j: (b, 0, j))],
        out_specs=pl.BlockSpec((1, Q_BLOCKS * TQ, ATT_W), lambda b, j: (b, j, 0)),
        out_shape=jax.ShapeDtypeStruct((B, S, ATT_W), BF16),
        scratch_shapes=[pltpu.VMEM((S, TQ), F32), pltpu.VMEM((S, TQ), BF16), pltpu.VMEM((S, W), F32)],
        compiler_params=pltpu.CompilerParams(dimension_semantics=("arbitrary", "arbitrary"),
                                             vmem_limit_bytes=VMEM_LIMIT),
        name="attn",
    )(qs, kka, kki, vt, gt)


def _segment_scan(x, op, identity, seg):
    lane = lax.broadcasted_iota(I32, x.shape, 1) % seg
    s = 1
    while s < seg:
        x = op(x, jnp.where(lane >= s, pltpu.roll(x, s, 1), identity))
        s *= 2
    return x


def _split3(x):
    hi = x.astype(BF16)
    r = x - hi.astype(F32)
    mid = r.astype(BF16)
    return hi, mid, (r - mid.astype(F32)).astype(BF16)


def _mlstm_kernel(nchunks, mq_ref, mk_ref, mv_ref, og_ref, gt_ref, ng_ref, y_ref, c_ref, g_ref):
    L, H, Dh = ML_L, ML_HEADS, ML_HEAD_DIM
    c_ref[...] = jnp.zeros_like(c_ref)
    r_i = lax.broadcasted_iota(I32, (L, L), 0)
    c_i = lax.broadcasted_iota(I32, (L, L), 1)
    eye = (r_i == c_i).astype(BF16)
    eye3 = jnp.concatenate([eye, eye, eye], axis=1)
    tril = r_i >= c_i
    ones_b = jnp.ones((L, LANES), BF16)
    ones2 = jnp.ones((2 * Dh, LANES), BF16)
    heads = range(H)

    gi = gt_ref[0, IDX_HEADS:IDX_HEADS + H, :]
    lf = gt_ref[0, IDX_HEADS + H:IDX_HEADS + 2 * H, :]
    b_seq = _segment_scan(lf, jnp.add, 0.0, L)
    a_seq = gi - b_seq
    g_ref[...] = jnp.concatenate([b_seq, a_seq, _segment_scan(a_seq, jnp.maximum, -jnp.inf, L),
                                  jnp.zeros_like(b_seq)], axis=0)

    def lane_sum(x):
        hi = x.astype(BF16)
        return _bdot(jnp.concatenate([hi, (x - hi.astype(F32)).astype(BF16)], axis=1), ones2)

    def chunk(c, m_prev):
        t0 = pl.multiple_of(c * L, L)
        rows = pl.ds(t0, L)
        stacked = g_ref[:, rows]
        a_all = stacked[H:2 * H]
        m_last = jnp.maximum(stacked[2 * H:3 * H, L - 1:L], m_prev)
        m_next = stacked[0:H, L - 1:L] + m_last
        cols = _dot_nt(eye3, jnp.concatenate(_split3(stacked), axis=1))
        b_col = [cols[:, h:h + 1] for h in heads]
        a_col = [cols[:, H + h:H + h + 1] for h in heads]
        mcol = [jnp.maximum(cols[:, 2 * H + h:2 * H + h + 1], m_prev[h:h + 1]) for h in heads]
        hsl = [slice(h * Dh, (h + 1) * Dh) for h in heads]
        q = [mq_ref[0, rows, hsl[h]] for h in heads]
        k = [mk_ref[0, rows, hsl[h]] for h in heads]
        v_aug = [jnp.concatenate([mv_ref[0, rows, hsl[h]], ones_b], axis=1) for h in heads]
        cst = [c_ref[h] for h in heads]
        qk = [_dot_nt(q[h], k[h]) for h in heads]
        qc = [_bdot(q[h], cst[h].astype(BF16)) for h in heads]
        w_intra = [jnp.where(tril, jnp.exp(a_all[h:h + 1] - mcol[h]), 0.0) for h in heads]
        sv = [_bdot((qk[h] * w_intra[h]).astype(BF16), v_aug[h]) for h in heads]
        for h in heads:
            wv = (v_aug[h].astype(F32) * jnp.exp(a_col[h] - m_last[h:h + 1])).astype(BF16)
            c_ref[h] = jnp.exp(m_prev[h:h + 1] - m_last[h:h + 1]) * cst[h] + _dot_tn(k[h], wv)
        haug = [jnp.exp(m_prev[h:h + 1] - mcol[h]) * qc[h] + sv[h] for h in heads]
        hm = [haug[h][:, :Dh] / jnp.maximum(jnp.abs(haug[h][:, Dh:]), jnp.exp(-(b_col[h] + mcol[h]))) for h in heads]
        d = [hm[h] - lane_sum(hm[h]) * (1.0 / Dh) for h in heads]
        var = [lane_sum(d[h] * d[h]) * (1.0 / Dh) for h in heads]
        for h in heads:
            yn = d[h] * lax.rsqrt(var[h] + LN_EPS) * ng_ref[:, hsl[h]]
            y_ref[0, rows, hsl[h]] = (og_ref[0, rows, hsl[h]].astype(F32) * yn).astype(y_ref.dtype)
        return m_next

    lax.fori_loop(0, nchunks, chunk, jnp.zeros((H, 1), F32), unroll=True)


def _mlstm_call(mq, mk, mv, og, gt, ng, B, S):
    seq = pl.BlockSpec((1, S, ML_W), lambda b: (b, 0, 0))
    return pl.pallas_call(
        functools.partial(_mlstm_kernel, S // ML_L),
        grid=(B,),
        in_specs=[seq, seq, seq, seq, pl.BlockSpec((1, 16, S), lambda b: (b, 0, 0)), _const_spec(ng.shape)],
        out_specs=seq,
        out_shape=jax.ShapeDtypeStruct((B, S, ML_W), BF16),
        scratch_shapes=[pltpu.VMEM((ML_HEADS, ML_HEAD_DIM, 2 * ML_HEAD_DIM), F32),
                        pltpu.VMEM((4 * ML_HEADS, S), F32)],
        compiler_params=pltpu.CompilerParams(dimension_semantics=("arbitrary",), vmem_limit_bytes=VMEM_LIMIT),
        name="mlstm",
    )(mq, mk, mv, og, gt, ng)


def _merge_kernel(x_ref, ya_ref, yb_ref, wg_ref, wua_ref, wub_ref, wo_ref, g_ref, b_ref, h_ref):
    x = x_ref[...]
    xb = x.astype(BF16)
    ma = jax.nn.sigmoid(_bdot(xb, wg_ref[:, :D_MODEL])) * _bdot(ya_ref[...], wua_ref[...])
    mb = jax.nn.sigmoid(_bdot(xb, wg_ref[:, D_MODEL:])) * _bdot(yb_ref[...], wub_ref[...])
    r = DEEPNORM_ALPHA * x + _bdot((ma + mb).astype(BF16), wo_ref[...])
    h_ref[...] = _layer_norm(r, g_ref[...], b_ref[...])


def _merge_call(x2, ya, yb, wg, wua, wub, wo, g, b):
    N = x2.shape[0]
    row = lambda w: pl.BlockSpec((TM_FFN, w), lambda i: (i, 0))
    return pl.pallas_call(
        _merge_kernel,
        grid=(N // TM_FFN,),
        in_specs=[row(D_MODEL), row(ATT_W), row(ML_W)] + [_const_spec(a.shape) for a in (wg, wua, wub, wo, g, b)],
        out_specs=row(D_MODEL),
        out_shape=jax.ShapeDtypeStruct((N, D_MODEL), F32),
        compiler_params=pltpu.CompilerParams(dimension_semantics=("arbitrary",), vmem_limit_bytes=VMEM_LIMIT),
        name="merge",
    )(x2, ya, yb, wg, wua, wub, wo, g, b)


def _ffn_kernel(h_ref, p_ref, w1_ref, w2_ref, wpg_ref, wpp_ref, g_ref, b_ref, o_ref):
    h = h_ref[...]
    hb = h.astype(BF16)
    ff = jnp.zeros_like(h)
    for c in range(D_FF // D_MODEL):
        cs = slice(c * D_MODEL, (c + 1) * D_MODEL)
        a = jnp.maximum(_bdot(hb, w1_ref[:, cs]), 0.0)
        ff = ff + _bdot((a * a).astype(BF16), w2_ref[cs, :])
    r = DEEPNORM_ALPHA * h + ff
    r = r + jax.nn.sigmoid(_bdot(r.astype(BF16), wpg_ref[...])) * _bdot(p_ref[...].astype(BF16), wpp_ref[...])
    o_ref[...] = _layer_norm(r, g_ref[...], b_ref[...])


def _ffn_call(h1, p2, w1, w2, wpg, wpp, g, b):
    N = h1.shape[0]
    row = lambda w: pl.BlockSpec((TM_FFN, w), lambda i: (i, 0))
    return pl.pallas_call(
        _ffn_kernel,
        grid=(N // TM_FFN,),
        in_specs=[row(D_MODEL), row(PLE_DIM)] + [_const_spec(a.shape) for a in (w1, w2, wpg, wpp, g, b)],
        out_specs=row(D_MODEL),
        out_shape=jax.ShapeDtypeStruct((N, D_MODEL), F32),
        compiler_params=pltpu.CompilerParams(dimension_semantics=("arbitrary",), vmem_limit_bytes=VMEM_LIMIT),
        name="ffn",
    )(h1, p2, w1, w2, wpg, wpp, g, b)


def _split_w_in_t(wt):
    parts, off = {}, 0
    for name, width in _SPLIT:
        parts[name] = wt[off:off + width]
        off += width
    return parts


def kernel(x, p, positions, w_in, conv_w, conv_b, b_igate, b_fgate, ml_norm_g, w_up_a, w_up_b, w_out,
           ln1_g, ln1_b, w_ff1, w_ff2, w_ple_gate, w_ple_proj, ln2_g, ln2_b):
    B, S, D = x.shape
    assert D == D_MODEL and S % TM == 0 and S % KC == 0 and w_in.shape[0] == DEPTH == 1
    N = B * S
    x2 = x.reshape(N, D)
    pos3 = positions.reshape(N // TM, 1, TM)
    inv_freq = 1.0 / (ROPE_THETA ** (jnp.arange(0, ATT_HEAD_DIM, 2, dtype=F32) / ATT_HEAD_DIM))
    invf = inv_freq.reshape(ATT_HEAD_DIM // 2, 1)

    w = _split_w_in_t(w_in[0].T)
    wt = jnp.concatenate([w['att_q'] * (ATT_HEAD_DIM ** -0.5 * LOG2E), w['idx_q'], w['att_k'], w['idx_k'],
                          w['att_v'], w['idx_w'], w['ml_i'], w['ml_f']], axis=0).astype(BF16)
    wn = jnp.concatenate([w['ml_q'], w['ml_k'], w['ml_v'], w['ml_o']], axis=0).astype(BF16).T
    gb = jnp.concatenate([jnp.zeros((IDX_HEADS,), F32), b_igate[0], b_fgate[0]]).reshape(16, 1)
    wg = jnp.concatenate([w['gate_a'], w['gate_b']], axis=0).astype(BF16).T

    qs, kka, kki, mq, mk, mv, og, vt, gt = _proj_call(x2, pos3, invf, wt, wn, conv_w[0], conv_b[0].reshape(1, -1),
                                                      gb, B, S)
    r3 = lambda a: a.reshape(B, S, a.shape[-1])
    ya = _attn_call(qs, r3(kka), r3(kki), vt, gt, B, S)
    yb = _mlstm_call(r3(mq), r3(mk), r3(mv), r3(og), gt, ml_norm_g[0].reshape(1, -1), B, S)
    h1 = _merge_call(x2, ya.reshape(N, ATT_W), yb.reshape(N, ML_W), wg, w_up_a[0].astype(BF16),
                     w_up_b[0].astype(BF16), w_out[0].astype(BF16), ln1_g[0].reshape(1, -1), ln1_b[0].reshape(1, -1))
    out = _ffn_call(h1, p[0].reshape(N, PLE_DIM), w_ff1[0].astype(BF16), w_ff2[0].astype(BF16),
                    w_ple_gate[0].astype(BF16), w_ple_proj[0].astype(BF16),
                    ln2_g[0].reshape(1, -1), ln2_b[0].reshape(1, -1))
    return out.reshape(B, S, D)
```

```python
import functools

import jax
import jax.numpy as jnp
from jax import lax
from jax.experimental import pallas as pl
from jax.experimental.pallas import tpu as pltpu

F32 = jnp.float32
BF16 = jnp.bfloat16
I32 = jnp.int32

D_MODEL = 1024
ATT_HEADS = 8
ATT_HEAD_DIM = 64
IDX_HEADS = 8
IDX_DIM = 64
TOPK_MAX = 256
ML_HEADS = 4
ML_HEAD_DIM = 128
CONV_WIDTH = 4
D_FF = 4 * D_MODEL
PLE_DIM = 256
ROPE_THETA = 10000.0
LN_EPS = 1e-5
DEPTH = 1
DEEPNORM_ALPHA = (2.0 * DEPTH) ** 0.25
IDX_W_SCALE = (IDX_HEADS ** -0.5) * (IDX_DIM ** -0.5)
LOG2E = 1.4426950408889634

ATT_W = ATT_HEADS * ATT_HEAD_DIM
IDX_QW = IDX_HEADS * IDX_DIM
ML_W = ML_HEADS * ML_HEAD_DIM
assert ATT_HEAD_DIM == IDX_DIM == 64 and ATT_HEADS == IDX_HEADS == 8

_SPLIT = (
    ('att_q', ATT_W), ('att_k', ATT_HEAD_DIM), ('att_v', ATT_HEAD_DIM),
    ('idx_q', IDX_QW), ('idx_k', IDX_DIM), ('idx_w', IDX_HEADS),
    ('ml_q', ML_W), ('ml_k', ML_W), ('ml_v', ML_W),
    ('ml_i', ML_HEADS), ('ml_f', ML_HEADS), ('ml_o', ML_W),
    ('gate_a', D_MODEL), ('gate_b', D_MODEL),
)

LANES = 128
V7X_VMEM_BYTES = 64 * 1024 * 1024
VMEM_LIMIT = V7X_VMEM_BYTES - 8 * 1024 * 1024

TM = 1024
TM_FFN = 1024
CONV_SLAB = 512
TQ = 256
Q_BLOCKS = 4
KC = 1024
MAX_REFINE = 160
assert KC == 4 * TQ and KC & (KC - 1) == 0
N_BISECT16 = 10
N_BISECT = 7
ML_L = 256


def _const_spec(shape):
    nd = len(shape)
    return pl.BlockSpec(shape, lambda *_: (0,) * nd, pipeline_mode=pl.Buffered(1))


def _bdot(a, b):
    return jnp.dot(a, b, preferred_element_type=F32)


def _dot_nt(a, b):
    return lax.dot_general(a, b, (((1,), (1,)), ((), ())), preferred_element_type=F32)


def _dot_tn(a, b):
    return lax.dot_general(a, b, (((0,), (0,)), ((), ())), preferred_element_type=F32)


def _layer_norm(r, g, b):
    mu = jnp.mean(r, axis=-1, keepdims=True)
    d = r - mu
    var = jnp.mean(d * d, axis=-1, keepdims=True)
    return d * lax.rsqrt(var + LN_EPS) * g + b


def _proj_kernel(tiles_per_seq, x_ref, pos_ref, invf_ref, wt_ref, wn_ref, cw_ref, cb_ref, gb_ref,
                 qs_ref, kka_ref, kki_ref, mq_ref, mk_ref, mv_ref, og_ref, vt_ref, gt_ref, ext_ref, halo_ref):
    i = pl.program_id(0)
    xb = x_ref[...].astype(BF16)
    HD, HALF = ATT_HEAD_DIM, ATT_HEAD_DIM // 2
    t = _dot_nt(wt_ref[...], xb)
    ang = invf_ref[...] * pos_ref[0].astype(F32)
    cos, sin = jnp.cos(ang), jnp.sin(ang)

    def rope(blk, ts):
        x1, x2 = t[blk * HD: blk * HD + HALF, ts], t[blk * HD + HALF: (blk + 1) * HD, ts]
        return x1 * cos[:, ts] - x2 * sin[:, ts], x2 * cos[:, ts] + x1 * sin[:, ts]

    is_att = lax.broadcasted_iota(I32, (1, LANES), 1) < HD
    for g in range(TM // TQ):
        ts = slice(g * TQ, (g + 1) * TQ)
        for h in range(ATT_HEADS):
            for r, piece in enumerate(rope(h, ts) + rope(ATT_HEADS + h, ts)):
                qs_ref[g, r * HALF:(r + 1) * HALF, h * TQ:(h + 1) * TQ] = piece.astype(BF16)
        kt = jnp.concatenate(rope(2 * ATT_HEADS, ts) + rope(2 * ATT_HEADS + 1, ts), axis=0).T
        kka_ref[ts, :] = jnp.where(is_att, kt, 0.0).astype(BF16)
        kki_ref[ts, :] = jnp.where(is_att, 0.0, kt).astype(BF16)

    c0 = 0
    not_first = i % tiles_per_seq != 0

    @pl.when(i == 0)
    def _():
        halo_ref[...] = jnp.zeros(halo_ref.shape, F32)

    for g in range(2 * ML_W // LANES // 2):
        cs = slice(c0 + g * 256, c0 + (g + 1) * 256)
        ls = slice(g * 256, (g + 1) * 256)
        ext_ref[0:8, :] = jnp.where(not_first, halo_ref[g], 0.0)
        ext_ref[8:, :] = _bdot(xb, wn_ref[:, cs])
        halo_ref[g] = ext_ref[TM:, :]
        for r0 in range(0, TM, CONV_SLAB):
            y = cb_ref[:, ls]
            for k in range(CONV_WIDTH):
                y = y + cw_ref[CONV_WIDTH - 1 - k: CONV_WIDTH - k, ls] * ext_ref[8 - k + r0: 8 - k + r0 + CONV_SLAB, :]
            y = y * jax.nn.sigmoid(y)
            rows = slice(r0, r0 + CONV_SLAB)
            if g < 2:
                mq_ref[rows, g * 256:(g + 1) * 256] = y.astype(mq_ref.dtype)
            else:
                mk_ref[rows, (g - 2) * 256:(g - 1) * 256] = (y * (ML_HEAD_DIM ** -0.5)).astype(mk_ref.dtype)
    c1 = c0 + 2 * ML_W
    mv_ref[...] = _bdot(xb, wn_ref[:, c1: c1 + ML_W]).astype(mv_ref.dtype)
    og_ref[...] = jax.nn.sigmoid(_bdot(xb, wn_ref[:, c1 + ML_W: c1 + 2 * ML_W])).astype(og_ref.dtype)

    r0 = (2 * ATT_HEADS + 2) * HD
    vt_ref[0] = t[r0: r0 + HD].astype(vt_ref.dtype)
    g = t[r0 + HD:] + gb_ref[...]
    row = lax.broadcasted_iota(I32, (16, 1), 0)
    gt_ref[0] = jnp.where(row < IDX_HEADS, g * IDX_W_SCALE,
                          jnp.where(row < IDX_HEADS + ML_HEADS, g,
                                    jnp.minimum(g, 0.0) - jnp.log1p(jnp.exp(-jnp.abs(g)))))


def _proj_call(x2, pos3, invf, wt, wn, cw, cb, gb, B, S):
    N = x2.shape[0]
    tps = S // TM
    W = ATT_HEADS * TQ
    row = lambda w: pl.BlockSpec((TM, w), lambda i: (i, 0))
    out_shape = [jax.ShapeDtypeStruct((N // TQ, LANES, W), BF16), jax.ShapeDtypeStruct((N, LANES), BF16),
                 jax.ShapeDtypeStruct((N, LANES), BF16)] + [jax.ShapeDtypeStruct((N, ML_W), BF16)] * 4 + [
        jax.ShapeDtypeStruct((B, ATT_HEAD_DIM, S), BF16), jax.ShapeDtypeStruct((B, 16, S), F32)]
    out_specs = [pl.BlockSpec((TM // TQ, LANES, W), lambda i: (i, 0, 0)), row(LANES), row(LANES)] + [row(ML_W)] * 4 + [
        pl.BlockSpec((1, ATT_HEAD_DIM, TM), lambda i: (i // tps, 0, i % tps)),
        pl.BlockSpec((1, 16, TM), lambda i: (i // tps, 0, i % tps))]
    return pl.pallas_call(
        functools.partial(_proj_kernel, tps),
        grid=(N // TM,),
        in_specs=[row(D_MODEL), pl.BlockSpec((1, 1, TM), lambda i: (i, 0, 0)), _const_spec(invf.shape),
                  _const_spec(wt.shape), _const_spec(wn.shape),
                  _const_spec(cw.shape), _const_spec(cb.shape), _const_spec(gb.shape)],
        out_specs=out_specs, out_shape=out_shape,
        scratch_shapes=[pltpu.VMEM((TM + 8, 256), F32), pltpu.VMEM((2 * ML_W // 256, 8, 256), F32)],
        compiler_params=pltpu.CompilerParams(dimension_semantics=("arbitrary",), vmem_limit_bytes=VMEM_LIMIT),
        name="proj",
    )(x2, pos3, invf, wt, wn, cw, cb, gb)


def _attn_kernel(k_sel, *refs):
    for slot in range(Q_BLOCKS):
        _attn_block(k_sel, slot, *refs)


def _attn_block(k_sel, slot, qs_ref, kka_ref, kki_ref, vt_ref, gq_ref, o_ref, sc_ref, scb_ref, s_ref):
    j = pl.program_id(1) * Q_BLOCKS + slot
    nkeys = (j + 1) * TQ
    qpos = j * TQ + lax.broadcasted_iota(I32, (1, TQ), 1)
    wq = gq_ref[0, 0:IDX_HEADS, slot * TQ:(slot + 1) * TQ]
    W = ATT_HEADS * TQ

    def for_chunks(body, carry):
        carry = lax.fori_loop(0, nkeys // KC, lambda c, cr: body(pl.multiple_of(c * KC, KC), KC, cr), carry)
        k0, size = (nkeys // KC) * KC, KC // 2
        while size >= TQ:
            present = (nkeys & size) != 0
            carry = lax.cond(present, lambda cr, k0=k0, size=size: body(pl.multiple_of(k0, size), size, cr),
                             lambda cr: cr, carry)
            k0, size = k0 + jnp.where(present, size, 0), size // 2
        return carry

    def fold(a, op):
        out = None
        for r0 in range(0, a.shape[0], 64):
            parts = [a[r0 + r * 8: r0 + (r + 1) * 8] for r in range(8)]
            while len(parts) > 1:
                parts = [op(parts[i], parts[i + 1]) for i in range(0, len(parts), 2)]
            out = parts[0] if out is None else op(out, parts[0])
        return out

    def phase_a(k0, size, carry):
        smin, smax = carry
        kc = kki_ref[0, pl.ds(k0, size), :]
        sc = jnp.zeros((size, TQ), F32)
        for h in range(IDX_HEADS):
            sc = sc + jnp.maximum(_bdot(kc, qs_ref[slot, :, h * TQ:(h + 1) * TQ]), 0.0) * wq[h:h + 1, :]
        causal = lax.broadcasted_iota(I32, (size, TQ), 0) + k0 <= qpos
        sc_ref[pl.ds(k0, size), :] = jnp.where(causal, sc, -jnp.inf)
        scb_ref[pl.ds(k0, size), :] = jnp.where(causal, sc, -jnp.inf).astype(BF16)
        smin = jnp.minimum(smin, fold(jnp.where(causal, sc, jnp.inf), jnp.minimum))
        smax = jnp.maximum(smax, fold(jnp.where(causal, sc, -jnp.inf), jnp.maximum))
        return smin, smax

    smin, smax = for_chunks(phase_a, (jnp.full((8, TQ), jnp.inf, F32), jnp.full((8, TQ), -jnp.inf, F32)))

    SUB = 64

    def count(pred):
        def body(k0, size, acc):
            for r0 in range(0, size, SUB):
                acc = acc + fold(pred(sc_ref[pl.ds(k0 + r0, SUB), :]).astype(I32), jnp.add)
            return acc
        return jnp.sum(for_chunks(body, jnp.zeros((8, TQ), I32)), axis=0, keepdims=True)

    def max_below(bound):
        def body(k0, size, acc):
            for r0 in range(0, size, SUB):
                s = sc_ref[pl.ds(k0 + r0, SUB), :]
                acc = jnp.maximum(acc, fold(jnp.where(s < bound, s, -jnp.inf), jnp.maximum))
            return acc
        return jnp.max(for_chunks(body, jnp.full((8, TQ), -jnp.inf, F32)), axis=0, keepdims=True)

    kq = jnp.minimum(qpos + 1, k_sel)

    def bisect(_, st):
        lo, hi, hi_ok, clo = st
        x = 0.5 * lo + 0.5 * hi
        c = count(lambda s: s >= x)
        ge = c >= kq
        return jnp.where(ge, x, lo), jnp.where(ge, hi, x), jnp.where(ge, hi_ok, 1), jnp.where(ge, c, clo)

    def count16(xb):
        one, zero = jnp.ones((16, TQ), BF16), jnp.zeros((16, TQ), BF16)

        def body(k0, size, acc):
            part = [zero, zero]
            for r0 in range(0, size, 4 * SUB):
                slab = scb_ref[pl.ds(k0 + r0, 4 * SUB), :]
                for n in range(4 * SUB // 16):
                    part[n % 2] = part[n % 2] + jnp.where(slab[n * 16:(n + 1) * 16] >= xb, one, zero)
            return acc + (part[0] + part[1]).astype(F32)
        return jnp.sum(for_chunks(body, jnp.zeros((16, TQ), F32)), axis=0, keepdims=True)

    def bisect16(_, st):
        lo, hi, hi_ok, clo = st
        xb = (0.5 * lo + 0.5 * hi).astype(BF16)
        x = xb.astype(F32)
        ge = count16(xb) >= kq.astype(F32)
        lo_new = jnp.where(ge, jnp.maximum(lo, x - jnp.maximum(jnp.abs(x) * 2.0 ** -7, 1e-30)), lo)
        return (lo_new, jnp.where(ge, hi, jnp.minimum(hi, x)), jnp.where(ge, hi_ok, 1),
                jnp.where(lo_new > lo, -1, clo))

    st = (jnp.min(smin, axis=0, keepdims=True), jnp.max(smax, axis=0, keepdims=True),
          jnp.zeros((1, TQ), I32), qpos + 1)
    st = lax.fori_loop(0, N_BISECT16, bisect16, st)
    st = lax.fori_loop(0, N_BISECT, bisect, st)

    def exact_check(st, tie):
        lo, hi, hi_ok, clo = st
        rem = (clo != kq) & (tie == 0)
        v = max_below(jnp.where(hi_ok > 0, hi, jnp.inf))
        cv = count(lambda s: s >= v)
        hit = rem & (cv >= kq)
        miss = rem & (cv < kq)
        st = (jnp.where(hit, v, lo), jnp.where(miss, v, hi), jnp.where(miss, 1, hi_ok), jnp.where(hit, kq, clo))
        return st, jnp.where(hit & (cv > kq), 1, tie)

    def flags(st, tie):
        return jnp.max(jnp.where((st[3] != kq) & (tie == 0), 2, 0) + tie)

    st, tie = exact_check(st, jnp.zeros((1, TQ), I32))

    def refine(carry):
        st, tie, _, trips = carry
        st, tie = exact_check(lax.fori_loop(0, 2, bisect, st), tie)
        return st, tie, flags(st, tie), trips + 1

    st, tie, code, _ = lax.while_loop(lambda c: (c[2] >= 2) & (c[3] < MAX_REFINE), refine,
                                      (st, tie, flags(st, tie), 0))
    thr = st[0]

    def tie_fix(_):
        need = (kq - count(lambda s: s > thr)).astype(F32)

        def body(k0, size, base):
            tri = lax.broadcasted_iota(I32, (size, size), 0) >= lax.broadcasted_iota(I32, (size, size), 1)
            s = sc_ref[pl.ds(k0, size), :]
            eq = (s == thr) & (tie > 0)
            rank = _bdot(tri.astype(BF16), eq.astype(BF16)) + base
            sc_ref[pl.ds(k0, size), :] = jnp.where(eq & (rank > need), -jnp.inf, s)
            return rank[size - 1:size, :]

        for_chunks(body, jnp.zeros((1, TQ), F32))
        return 0

    lax.cond(code % 2 == 1, tie_fix, lambda _: 0, 0)

    def logits(k0, size, mx):
        kc = kka_ref[0, pl.ds(k0, size), :]
        bias = jnp.where(sc_ref[pl.ds(k0, size), :] >= thr, 0.0, -jnp.inf)
        cols = []
        for h in range(ATT_HEADS):
            sh = _bdot(kc, qs_ref[slot, :, h * TQ:(h + 1) * TQ]) + bias
            s_ref[pl.ds(k0, size), h * TQ:(h + 1) * TQ] = sh
            cols.append(fold(sh, jnp.maximum))
        return jnp.maximum(mx, jnp.concatenate(cols, axis=1))

    mx = for_chunks(logits, jnp.full((8, W), -jnp.inf, F32))
    mx = jnp.max(mx, axis=0, keepdims=True)

    def weighted(k0, size, acc):
        vc = jnp.concatenate([vt_ref[0, :, pl.ds(k0, size)], jnp.ones((8, size), BF16)], axis=0)
        pv = []
        for h in range(ATT_HEADS):
            hs = slice(h * TQ, (h + 1) * TQ)
            p = jnp.exp2(s_ref[pl.ds(k0, size), hs] - mx[:, hs])
            pv.append(_bdot(vc, p.astype(BF16)))
        return acc + jnp.concatenate(pv, axis=1)

    acc = for_chunks(weighted, jnp.zeros((ATT_HEAD_DIM + 8, W), F32))
    out_t = acc[:ATT_HEAD_DIM] / acc[ATT_HEAD_DIM:ATT_HEAD_DIM + 1]
    out_t = jnp.concatenate([out_t[:, h * TQ:(h + 1) * TQ] for h in range(ATT_HEADS)], axis=0)
    o_ref[0, slot * TQ:(slot + 1) * TQ, :] = out_t.T.astype(o_ref.dtype)


def _attn_call(qs, kka, kki, vt, gt, B, S):
    k_sel = min(TOPK_MAX, S // 4)
    W = ATT_HEADS * TQ
    nstep = S // (Q_BLOCKS * TQ)
    kspec = pl.BlockSpec((1, S, LANES), lambda b, j: (b, 0, 0))
    return pl.pallas_call(
        functools.partial(_attn_kernel, k_sel),
        grid=(B, nstep),
        in_specs=[pl.BlockSpec((Q_BLOCKS, LANES, W), lambda b, j: (b * nstep + j, 0, 0)), kspec, kspec,
                  pl.BlockSpec((1, ATT_HEAD_DIM, S), lambda b, j: (b, 0, 0)),
                  pl.BlockSpec((1, 16, Q_BLOCKS * TQ), lambda b, j: (b, 0, j))],
        out_specs=pl.BlockSpec((1, Q_BLOCKS * TQ, ATT_W), lambda b, j: (b, j, 0)),
        out_shape=jax.ShapeDtypeStruct((B, S, ATT_W), BF16),
        scratch_shapes=[pltpu.VMEM((S, TQ), F32), pltpu.VMEM((S, TQ), BF16), pltpu.VMEM((S, W), F32)],
        compiler_params=pltpu.CompilerParams(dimension_semantics=("arbitrary", "arbitrary"),
                                             vmem_limit_bytes=VMEM_LIMIT),
        name="attn",
    )(qs, kka, kki, vt, gt)


def _segment_scan(x, op, identity, seg):
    lane = lax.broadcasted_iota(I32, x.shape, 1) % seg
    s = 1
    while s < seg:
        x = op(x, jnp.where(lane >= s, pltpu.roll(x, s, 1), identity))
        s *= 2
    return x


def _split3(x):
    hi = x.astype(BF16)
    r = x - hi.astype(F32)
    mid = r.astype(BF16)
    return hi, mid, (r - mid.astype(F32)).astype(BF16)


def _mlstm_kernel(nchunks, mq_ref, mk_ref, mv_ref, og_ref, gt_ref, ng_ref, y_ref, c_ref, g_ref):
    L, H, Dh = ML_L, ML_HEADS, ML_HEAD_DIM
    c_ref[...] = jnp.zeros_like(c_ref)
    r_i = lax.broadcasted_iota(I32, (L, L), 0)
    c_i = lax.broadcasted_iota(I32, (L, L), 1)
    eye = (r_i == c_i).astype(BF16)
    eye3 = jnp.concatenate([eye, eye, eye], axis=1)
    tril = r_i >= c_i
    ones_b = jnp.ones((L, LANES), BF16)
    ones2 = jnp.ones((2 * Dh, LANES), BF16)
    heads = range(H)

    gi = gt_ref[0, IDX_HEADS:IDX_HEADS + H, :]
    lf = gt_ref[0, IDX_HEADS + H:IDX_HEADS + 2 * H, :]
    b_seq = _segment_scan(lf, jnp.add, 0.0, L)
    a_seq = gi - b_seq
    g_ref[...] = jnp.concatenate([b_seq, a_seq, _segment_scan(a_seq, jnp.maximum, -jnp.inf, L),
                                  jnp.zeros_like(b_seq)], axis=0)

    def lane_sum(x):
        hi = x.astype(BF16)
        return _bdot(jnp.concatenate([hi, (x - hi.astype(F32)).astype(BF16)], axis=1), ones2)

    def chunk(c, m_prev):
        t0 = pl.multiple_of(c * L, L)
        rows = pl.ds(t0, L)
        stacked = g_ref[:, rows]
        a_all = stacked[H:2 * H]
        m_last = jnp.maximum(stacked[2 * H:3 * H, L - 1:L], m_prev)
        m_next = stacked[0:H, L - 1:L] + m_last
        cols = _dot_nt(eye3, jnp.concatenate(_split3(stacked), axis=1))
        b_col = [cols[:, h:h + 1] for h in heads]
        a_col = [cols[:, H + h:H + h + 1] for h in heads]
        mcol = [jnp.maximum(cols[:, 2 * H + h:2 * H + h + 1], m_prev[h:h + 1]) for h in heads]
        hsl = [slice(h * Dh, (h + 1) * Dh) for h in heads]
        q = [mq_ref[0, rows, hsl[h]] for h in heads]
        k = [mk_ref[0, rows, hsl[h]] for h in heads]
        v_aug = [jnp.concatenate([mv_ref[0, rows, hsl[h]], ones_b], axis=1) for h in heads]
        cst = [c_ref[h] for h in heads]
        qk = [_dot_nt(q[h], k[h]) for h in heads]
        qc = [_bdot(q[h], cst[h].astype(BF16)) for h in heads]
        w_intra = [jnp.where(tril, jnp.exp(a_all[h:h + 1] - mcol[h]), 0.0) for h in heads]
        sv = [_bdot((qk[h] * w_intra[h]).astype(BF16), v_aug[h]) for h in heads]
        for h in heads:
            wv = (v_aug[h].astype(F32) * jnp.exp(a_col[h] - m_last[h:h + 1])).astype(BF16)
            c_ref[h] = jnp.exp(m_prev[h:h + 1] - m_last[h:h + 1]) * cst[h] + _dot_tn(k[h], wv)
        haug = [jnp.exp(m_prev[h:h + 1] - mcol[h]) * qc[h] + sv[h] for h in heads]
        hm = [haug[h][:, :Dh] / jnp.maximum(jnp.abs(haug[h][:, Dh:]), jnp.exp(-(b_col[h] + mcol[h]))) for h in heads]
        d = [hm[h] - lane_sum(hm[h]) * (1.0 / Dh) for h in heads]
        var = [lane_sum(d[h] * d[h]) * (1.0 / Dh) for h in heads]
        for h in heads:
            yn = d[h] * lax.rsqrt(var[h] + LN_EPS) * ng_ref[:, hsl[h]]
            y_ref[0, rows, hsl[h]] = (og_ref[0, rows, hsl[h]].astype(F32) * yn).astype(y_ref.dtype)
        return m_next

    lax.fori_loop(0, nchunks, chunk, jnp.zeros((H, 1), F32), unroll=True)


def _mlstm_call(mq, mk, mv, og, gt, ng, B, S):
    seq = pl.BlockSpec((1, S, ML_W), lambda b: (b, 0, 0))
    return pl.pallas_call(
        functools.partial(_mlstm_kernel, S // ML_L),
        grid=(B,),
        in_specs=[seq, seq, seq, seq, pl.BlockSpec((1, 16, S), lambda b: (b, 0, 0)), _const_spec(ng.shape)],
        out_specs=seq,
        out_shape=jax.ShapeDtypeStruct((B, S, ML_W), BF16),
        scratch_shapes=[pltpu.VMEM((ML_HEADS, ML_HEAD_DIM, 2 * ML_HEAD_DIM), F32),
                        pltpu.VMEM((4 * ML_HEADS, S), F32)],
        compiler_params=pltpu.CompilerParams(dimension_semantics=("arbitrary",), vmem_limit_bytes=VMEM_LIMIT),
        name="mlstm",
    )(mq, mk, mv, og, gt, ng)


def _merge_kernel(x_ref, ya_ref, yb_ref, wg_ref, wua_ref, wub_ref, wo_ref, g_ref, b_ref, h_ref):
    x = x_ref[...]
    xb = x.astype(BF16)
    ma = jax.nn.sigmoid(_bdot(xb, wg_ref[:, :D_MODEL])) * _bdot(ya_ref[...], wua_ref[...])
    mb = jax.nn.sigmoid(_bdot(xb, wg_ref[:, D_MODEL:])) * _bdot(yb_ref[...], wub_ref[...])
    r = DEEPNORM_ALPHA * x + _bdot((ma + mb).astype(BF16), wo_ref[...])
    h_ref[...] = _layer_norm(r, g_ref[...], b_ref[...])


def _merge_call(x2, ya, yb, wg, wua, wub, wo, g, b):
    N = x2.shape[0]
    row = lambda w: pl.BlockSpec((TM_FFN, w), lambda i: (i, 0))
    return pl.pallas_call(
        _merge_kernel,
        grid=(N // TM_FFN,),
        in_specs=[row(D_MODEL), row(ATT_W), row(ML_W)] + [_const_spec(a.shape) for a in (wg, wua, wub, wo, g, b)],
        out_specs=row(D_MODEL),
        out_shape=jax.ShapeDtypeStruct((N, D_MODEL), F32),
        compiler_params=pltpu.CompilerParams(dimension_semantics=("arbitrary",), vmem_limit_bytes=VMEM_LIMIT),
        name="merge",
    )(x2, ya, yb, wg, wua, wub, wo, g, b)


def _ffn_kernel(h_ref, p_ref, w1_ref, w2_ref, wpg_ref, wpp_ref, g_ref, b_ref, o_ref):
    h = h_ref[...]
    hb = h.astype(BF16)
    ff = jnp.zeros_like(h)
    for c in range(D_FF // D_MODEL):
        cs = slice(c * D_MODEL, (c + 1) * D_MODEL)
        a = jnp.maximum(_bdot(hb, w1_ref[:, cs]), 0.0)
        ff = ff + _bdot((a * a).astype(BF16), w2_ref[cs, :])
    r = DEEPNORM_ALPHA * h + ff
    r = r + jax.nn.sigmoid(_bdot(r.astype(BF16), wpg_ref[...])) * _bdot(p_ref[...].astype(BF16), wpp_ref[...])
    o_ref[...] = _layer_norm(r, g_ref[...], b_ref[...])


def _ffn_call(h1, p2, w1, w2, wpg, wpp, g, b):
    N = h1.shape[0]
    row = lambda w: pl.BlockSpec((TM_FFN, w), lambda i: (i, 0))
    return pl.pallas_call(
        _ffn_kernel,
        grid=(N // TM_FFN,),
        in_specs=[row(D_MODEL), row(PLE_DIM)] + [_const_spec(a.shape) for a in (w1, w2, wpg, wpp, g, b)],
        out_specs=row(D_MODEL),
        out_shape=jax.ShapeDtypeStruct((N, D_MODEL), F32),
        compiler_params=pltpu.CompilerParams(dimension_semantics=("arbitrary",), vmem_limit_bytes=VMEM_LIMIT),
        name="ffn",
    )(h1, p2, w1, w2, wpg, wpp, g, b)


def _split_w_in_t(wt):
    parts, off = {}, 0
    for name, width in _SPLIT:
        parts[name] = wt[off:off + width]
        off += width
    return parts


def kernel(x, p, positions, w_in, conv_w, conv_b, b_igate, b_fgate, ml_norm_g, w_up_a, w_up_b, w_out,
           ln1_g, ln1_b, w_ff1, w_ff2, w_ple_gate, w_ple_proj, ln2_g, ln2_b):
    B, S, D = x.shape
    assert D == D_MODEL and S % TM == 0 and S % KC == 0 and w_in.shape[0] == DEPTH == 1
    N = B * S
    x2 = x.reshape(N, D)
    pos3 = positions.reshape(N // TM, 1, TM)
    inv_freq = 1.0 / (ROPE_THETA ** (jnp.arange(0, ATT_HEAD_DIM, 2, dtype=F32) / ATT_HEAD_DIM))
    invf = inv_freq.reshape(ATT_HEAD_DIM // 2, 1)

    w = _split_w_in_t(w_in[0].T)
    wt = jnp.concatenate([w['att_q'] * (ATT_HEAD_DIM ** -0.5 * LOG2E), w['idx_q'], w['att_k'], w['idx_k'],
                          w['att_v'], w['idx_w'], w['ml_i'], w['ml_f']], axis=0).astype(BF16)
    wn = jnp.concatenate([w['ml_q'], w['ml_k'], w['ml_v'], w['ml_o']], axis=0).astype(BF16).T
    gb = jnp.concatenate([jnp.zeros((IDX_HEADS,), F32), b_igate[0], b_fgate[0]]).reshape(16, 1)
    wg = jnp.concatenate([w['gate_a'], w['gate_b']], axis=0).astype(BF16).T

    qs, kka, kki, mq, mk, mv, og, vt, gt = _proj_call(x2, pos3, invf, wt, wn, conv_w[0], conv_b[0].reshape(1, -1),
                                                      gb, B, S)
    r3 = lambda a: a.reshape(B, S, a.shape[-1])
    ya = _attn_call(qs, r3(kka), r3(kki), vt, gt, B, S)
    yb = _mlstm_call(r3(mq), r3(mk), r3(mv), r3(og), gt, ml_norm_g[0].reshape(1, -1), B, S)
    h1 = _merge_call(x2, ya.reshape(N, ATT_W), yb.reshape(N, ML_W), wg, w_up_a[0].astype(BF16),
                     w_up_b[0].astype(BF16), w_out[0].astype(BF16), ln1_g[0].reshape(1, -1), ln1_b[0].reshape(1, -1))
    out = _ffn_call(h1, p[0].reshape(N, PLE_DIM), w_ff1[0].astype(BF16), w_ff2[0].astype(BF16),
                    w_ple_gate[0].astype(BF16), w_ple_proj[0].astype(BF16),
                    ln2_g[0].reshape(1, -1), ln2_b[0].reshape(1, -1))
    return out.reshape(B, S, D)
```
